```python
import jax, jax.numpy as jnp
from jax import lax
import numpy as np

D_MODEL = 2048
BATCH = 16
SEQ = 256
DEPTH = 4
DEC_BATCH = 8
DEC_SEQ = 1024
PAST_LEN = 256

GRID_W = 64
N_MIXERS = 3
HEAD_DIM = 128
N_HEADS = D_MODEL // HEAD_DIM
N_KV_HEADS = 4
KV_REP = N_HEADS // N_KV_HEADS
ATTN_SCALE = HEAD_DIM ** -0.5
WINDOW = 128
WIN_BLK = 128
CTX_BLK = 128
NA_ROWS = 8
NA_COLS = 16
NA_QCOLS = 16
NA_CB = GRID_W // NA_QCOLS
NA_KSPAN = 2 * NA_COLS
ROPE_BASE = 10000.0
ROPE_FREQS = HEAD_DIM // 4
CHUNK = 128
SG_DIM = D_MODEL
SG_GROUPS = 16
SG_CH = SG_DIM // SG_GROUPS
N_EXPERTS = 16
EC_FACTOR = 2
D_EXPERT = 1024
NORM_EPS = 1e-6
NEG_INF = -1e30
N_A_LAYERS = (DEPTH + 2) // 3
N_B_LAYERS = (DEPTH + 1) // 3
N_C_LAYERS = DEPTH // 3

kernel_name = 'hybrid_diffusion_prefix_trunk_step'


def _rmsnorm(x, g):
    xf = x.astype(jnp.float32)
    y = xf * lax.rsqrt(jnp.mean(xf * xf, axis=-1, keepdims=True) + NORM_EPS)
    return (y * g.astype(jnp.float32)).astype(x.dtype)


def _adaln(cond, w, b):
    m = (jax.nn.silu(cond) @ w + b).reshape(cond.shape[0], 1, 6, D_MODEL)
    return [m[:, :, i] for i in range(6)]


def _modulate(x, g, shift, scale):
    return _rmsnorm(x, g) * (1 + scale) + shift


def _joint_softmax(logits, sink=None):
    parts = ([sink] if sink is not None else []) + list(logits)
    cuts = [int(s) for s in np.cumsum([p.shape[-1] for p in parts])[:-1]]
    probs = jnp.split(jax.nn.softmax(jnp.concatenate(parts, axis=-1), axis=-1), cuts, axis=-1)
    return probs[1:] if sink is not None else probs


def _axial_rope_tables(n):
    t = jnp.arange(n)
    row = (t // GRID_W).astype(jnp.float32)
    col = (t % GRID_W).astype(jnp.float32)
    inv = ROPE_BASE ** (-jnp.arange(ROPE_FREQS, dtype=jnp.float32) / ROPE_FREQS)
    ang = jnp.stack([row[:, None] * inv, col[:, None] * inv], axis=1)
    return jnp.cos(ang), jnp.sin(ang)


def _rope(x, cos, sin):
    shp = x.shape
    x5 = x.reshape(shp[:-1] + (2, 2, ROPE_FREQS))
    a, b = x5[..., 0, :], x5[..., 1, :]
    bshape = (cos.shape[0],) + (1,) * (x.ndim - 3) + (2, ROPE_FREQS)
    cc = cos.reshape(bshape).astype(x.dtype)
    ss = sin.reshape(bshape).astype(x.dtype)
    return jnp.stack([a * cc - b * ss, b * cc + a * ss], axis=-2).reshape(shp)


def _dense_block_attention(q, k, v, sink):
    b, s, g, r, dh = q.shape
    nb = s // CTX_BLK
    qb = jnp.moveaxis(q.reshape(b, nb, CTX_BLK, g, r, dh), 1, 0)

    def step(qi):
        sc = jnp.einsum('bqgrd,bkgd->bgrqk', qi, k).astype(jnp.float32) * ATTN_SCALE
        sk = None if sink is None else jnp.broadcast_to(
            sink.astype(jnp.float32)[None, :, :, None, None], sc.shape[:-1] + (1,))
        (p,) = _joint_softmax([sc], sk)
        return jnp.einsum('bgrqk,bkgd->bqgrd', p.astype(v.dtype), v)

    o = lax.map(step, qb)
    return jnp.moveaxis(o, 0, 1).reshape(b, s, g, r, dh)


def _gqa_project(h, w_qkv):
    b, n, _ = h.shape
    qkv = h @ w_qkv
    nq, nk = N_HEADS * HEAD_DIM, N_KV_HEADS * HEAD_DIM
    q = qkv[..., :nq].reshape(b, n, N_KV_HEADS, KV_REP, HEAD_DIM)
    k = qkv[..., nq:nq + nk].reshape(b, n, N_KV_HEADS, HEAD_DIM)
    v = qkv[..., nq + nk:].reshape(b, n, N_KV_HEADS, HEAD_DIM)
    return q, k, v


def _banded_window_attention(q, k, v, kc, vc, sink):
    b, n, g, r, dh = q.shape
    nb = n // WIN_BLK
    span = WIN_BLK + 2 * WINDOW
    kp = jnp.pad(k, ((0, 0), (WINDOW, WINDOW), (0, 0), (0, 0)))
    vp = jnp.pad(v, ((0, 0), (WINDOW, WINDOW), (0, 0), (0, 0)))
    qb = jnp.moveaxis(q.reshape(b, nb, WIN_BLK, g, r, dh), 1, 0)
    band = np.abs(np.arange(WIN_BLK)[:, None] + WINDOW - np.arange(span)[None, :]) <= WINDOW
    sink_f = sink.astype(jnp.float32)[None, :, :, None, None]

    def step(args):
        qi, bi = args
        start = bi * WIN_BLK
        kw = lax.dynamic_slice_in_dim(kp, start, span, axis=1)
        vw = lax.dynamic_slice_in_dim(vp, start, span, axis=1)
        kpos = start - WINDOW + jnp.arange(span)
        valid = band & ((kpos >= 0) & (kpos < n))[None, :]
        s_w = jnp.einsum('bqgrd,bkgd->bgrqk', qi, kw).astype(jnp.float32) * ATTN_SCALE
        s_w = jnp.where(valid, s_w, NEG_INF)
        s_c = jnp.einsum('bqgrd,bkgd->bgrqk', qi, kc).astype(jnp.float32) * ATTN_SCALE
        sk = jnp.broadcast_to(sink_f, s_w.shape[:-1] + (1,))
        p_c, p_w = _joint_softmax([s_c, s_w], sk)
        return (jnp.einsum('bgrqk,bkgd->bqgrd', p_c.astype(vc.dtype), vc)
                + jnp.einsum('bgrqk,bkgd->bqgrd', p_w.astype(vw.dtype), vw))

    o = lax.map(step, (qb, jnp.arange(nb)))
    return jnp.moveaxis(o, 0, 1).reshape(b, n, g, r, dh)


def _mixer_a_context(h, w_qkv, sink, w_o):
    b, n, _ = h.shape
    q, k, v = _gqa_project(h, w_qkv)
    o = _dense_block_attention(q, k, v, sink.reshape(N_KV_HEADS, KV_REP))
    return o.reshape(b, n, D_MODEL) @ w_o, k, v


def _mixer_a_latent(h, kc, vc, w_qkv, sink, w_o):
    b, n, _ = h.shape
    q, k, v = _gqa_project(h, w_qkv)
    cos, sin = _axial_rope_tables(n)
    q = _rope(q, cos, sin)
    k = _rope(k, cos, sin)
    o = _banded_window_attention(q, k, v, kc, vc, sink.reshape(N_KV_HEADS, KV_REP))
    return o.reshape(b, n, D_MODEL) @ w_o


def _mha_project(h, w_qkv):
    b, n, _ = h.shape
    qkv = (h @ w_qkv).reshape(b, n, 3, N_HEADS, HEAD_DIM)
    return qkv[:, :, 0], qkv[:, :, 1], qkv[:, :, 2]


def _mixer_b_context(h, w_qkv, w_o):
    b, n, _ = h.shape
    q, k, v = _mha_project(h, w_qkv)
    o = _dense_block_attention(q[:, :, :, None], k, v, None)
    return o.reshape(b, n, D_MODEL) @ w_o, k, v


def _mixer_b_latent(h, kc, vc, w_qkv, rpb, w_o):
    b, n, _ = h.shape
    rows = n // GRID_W
    kh = min(NA_ROWS, rows)
    q, k, v = _mha_project(h, w_qkv)
    kg = k.reshape(b, rows, GRID_W, N_HEADS, HEAD_DIM)
    vg = v.reshape(b, rows, GRID_W, N_HEADS, HEAD_DIM)
    qg = jnp.moveaxis(q.reshape(b, rows, NA_CB, NA_QCOLS, N_HEADS, HEAD_DIM), 1, 0)
    qcol = np.arange(NA_CB)[:, None] * NA_QCOLS + np.arange(NA_QCOLS)[None, :]
    cs = np.clip(qcol - NA_COLS // 2, 0, GRID_W - NA_COLS)
    kstart = np.clip(np.arange(NA_CB) * NA_QCOLS - NA_COLS // 2, 0, GRID_W - NA_KSPAN)
    kcol = kstart[:, None] + np.arange(NA_KSPAN)[None, :]
    colmask = (kcol[:, None, :] >= cs[..., None]) & (kcol[:, None, :] < cs[..., None] + NA_COLS)
    dc_idx = np.clip(kcol[:, None, :] - qcol[..., None] + NA_COLS - 1, 0, 2 * NA_COLS - 2)
    rpb_c = rpb[:, :, dc_idx]
    mask = colmask[:, :, None, :]

    def row_step(args):
        qr, r = args
        rs = jnp.clip(r - kh // 2, 0, rows - kh)
        kr = lax.dynamic_slice_in_dim(kg, rs, kh, axis=1)[:, :, kcol]
        vr = lax.dynamic_slice_in_dim(vg, rs, kh, axis=1)[:, :, kcol]
        dr_idx = rs + jnp.arange(kh) - r + NA_ROWS - 1
        bias = jnp.take(rpb_c, dr_idx, axis=1).transpose(0, 2, 3, 1, 4).astype(jnp.float32)
        s_n = jnp.einsum('bcqhd,bjckhd->bhcqjk', qr, kr).astype(jnp.float32) * ATTN_SCALE + bias
        s_n = jnp.where(mask, s_n, NEG_INF).reshape(s_n.shape[:4] + (kh * NA_KSPAN,))
        s_c = jnp.einsum('bcqhd,bkhd->bhcqk', qr, kc).astype(jnp.float32) * ATTN_SCALE
        p_c, p_n = _joint_softmax([s_c, s_n])
        p_n = p_n.reshape(p_n.shape[:4] + (kh, NA_KSPAN))
        return (jnp.einsum('bhcqk,bkhd->bcqhd', p_c.astype(vc.dtype), vc)
                + jnp.einsum('bhcqjk,bjckhd->bcqhd', p_n.astype(vr.dtype), vr))

    o = lax.map(row_step, (qg, jnp.arange(rows)))
    return jnp.moveaxis(o, 0, 1).reshape(b, n, D_MODEL) @ w_o


def _mixer_c(h, w_in, v_g, w_s, b_s, w_out):
    b, n, _ = h.shape
    z = jax.nn.gelu(h @ w_in)
    u, v = z[..., :SG_DIM], _rmsnorm(z[..., SG_DIM:], v_g)
    vb = v.reshape(b, n // CHUNK, CHUNK, SG_GROUPS, SG_CH)
    f = jnp.einsum('gpq,bnqgc->bnpgc', w_s, vb) + b_s.T[None, None, :, :, None]
    return (u * f.reshape(b, n, SG_DIM)) @ w_out


def _expert_choice_moe(h, w_router, w_gate, w_up, w_down):
    b, t, d = h.shape
    cap = EC_FACTOR * t // N_EXPERTS
    aff = jax.nn.softmax(jnp.einsum('btd,de->bte', h, w_router).astype(jnp.float32), axis=-1)
    gate, idx = lax.top_k(jnp.swapaxes(aff, 1, 2), cap)
    xe = jax.vmap(lambda hb, ib: hb[ib])(h, idx)
    a = jnp.einsum('becd,edf->becf', xe, w_gate)
    u = jnp.einsum('becd,edf->becf', xe, w_up)
    y = jnp.einsum('becf,efd->becd', jax.nn.silu(a) * u, w_down) * gate[..., None].astype(h.dtype)
    return jax.vmap(lambda ib, yb: jnp.zeros((t, d), yb.dtype).at[ib.reshape(-1)].add(yb.reshape(-1, d)))(idx, y)


def setup_inputs(seed: int = 0) -> dict:
    key = jax.random.key(seed)
    ks = jax.random.split(key, 32)

    def nrm(k, shape, scale=1.0):
        return jax.random.normal(k, shape, jnp.float32) * scale

    d = D_MODEL
    qkv_a = (N_HEADS + 2 * N_KV_HEADS) * HEAD_DIM
    return {
        'x_prompt': nrm(ks[0], (BATCH, SEQ, d)),
        'x_sample': nrm(ks[1], (DEC_BATCH, DEC_SEQ, d)),
        'cache_k_0': nrm(ks[2], (DEC_BATCH, PAST_LEN, N_KV_HEADS, HEAD_DIM)),
        'cache_v_0': nrm(ks[3], (DEC_BATCH, PAST_LEN, N_KV_HEADS, HEAD_DIM)),
        'cache_k_1': nrm(ks[4], (DEC_BATCH, PAST_LEN, N_HEADS, HEAD_DIM)),
        'cache_v_1': nrm(ks[5], (DEC_BATCH, PAST_LEN, N_HEADS, HEAD_DIM)),
        'cache_k_3': nrm(ks[6], (DEC_BATCH, PAST_LEN, N_KV_HEADS, HEAD_DIM)),
        'cache_v_3': nrm(ks[7], (DEC_BATCH, PAST_LEN, N_KV_HEADS, HEAD_DIM)),
        'c': nrm(ks[8], (DEC_BATCH, d)),
        'c_ctx': nrm(ks[9], (d,)),
        'norm1_g': 1.0 + nrm(ks[10], (DEPTH, d), 0.02),
        'norm2_g': 1.0 + nrm(ks[11], (DEPTH, d), 0.02),
        'w_mod': nrm(ks[12], (DEPTH, d, 6 * d), d ** -0.5),
        'b_mod': nrm(ks[13], (DEPTH, 6 * d), 0.02),
        'a_w_qkv': nrm(ks[14], (N_A_LAYERS, d, qkv_a), d ** -0.5),
        'a_sink': nrm(ks[15], (N_A_LAYERS, N_HEADS), 0.5),
        'a_w_o': nrm(ks[16], (N_A_LAYERS, d, d), d ** -0.5),
        'b_w_qkv': nrm(ks[17], (N_B_LAYERS, d, 3 * d), d ** -0.5),
        'b_rpb': nrm(ks[18], (N_B_LAYERS, N_HEADS, 2 * NA_ROWS - 1, 2 * NA_COLS - 1), 0.1),
        'b_w_o': nrm(ks[19], (N_B_LAYERS, d, d), d ** -0.5),
        'c_w_in': nrm(ks[20], (N_C_LAYERS, d, 2 * SG_DIM), d ** -0.5),
        'c_v_g': 1.0 + nrm(ks[21], (N_C_LAYERS, SG_DIM), 0.02),
        'c_w_s': nrm(ks[22], (N_C_LAYERS, SG_GROUPS, CHUNK, CHUNK), CHUNK ** -0.5),
        'c_b_s': 1.0 + nrm(ks[23], (N_C_LAYERS, SG_GROUPS, CHUNK), 0.1),
        'c_w_out': nrm(ks[24], (N_C_LAYERS, SG_DIM, d), SG_DIM ** -0.5),
        'w_router': nrm(ks[25], (DEPTH, d, N_EXPERTS), d ** -0.5),
        'w_gate': nrm(ks[26], (DEPTH, N_EXPERTS, d, D_EXPERT), d ** -0.5),
        'w_up': nrm(ks[27], (DEPTH, N_EXPERTS, d, D_EXPERT), d ** -0.5),
        'w_down': nrm(ks[28], (DEPTH, N_EXPERTS, D_EXPERT, d), D_EXPERT ** -0.5),
        'final_g': 1.0 + nrm(ks[29], (d,), 0.02),
    }


def reference(x_prompt, x_sample, cache_k_0, cache_v_0, cache_k_1, cache_v_1, cache_k_3, cache_v_3,
              c, c_ctx, norm1_g, norm2_g, w_mod, b_mod, a_w_qkv, a_sink, a_w_o, b_w_qkv, b_rpb, b_w_o,
              c_w_in, c_v_g, c_w_s, c_b_s, c_w_out, w_router, w_gate, w_up, w_down, final_g):
    ctx_cache = {0: (cache_k_0, cache_v_0), 1: (cache_k_1, cache_v_1), 3: (cache_k_3, cache_v_3)}
    xp, xs = x_prompt, x_sample
    new_state = []
    for l in range(DEPTH):
        kind, j = l % N_MIXERS, l // N_MIXERS
        mp = _adaln(c_ctx[None], w_mod[l], b_mod[l])
        ms = _adaln(c, w_mod[l], b_mod[l])
        hp = _modulate(xp, norm1_g[l], mp[0], mp[1])
        hs = _modulate(xs, norm1_g[l], ms[0], ms[1])
        if kind == 0:
            kc, vc = ctx_cache[l]
            op, k_new, v_new = _mixer_a_context(hp, a_w_qkv[j], a_sink[j], a_w_o[j])
            os_ = _mixer_a_latent(hs, kc, vc, a_w_qkv[j], a_sink[j], a_w_o[j])
            new_state += [k_new, v_new]
        elif kind == 1:
            kc, vc = ctx_cache[l]
            op, k_new, v_new = _mixer_b_context(hp, b_w_qkv[j], b_w_o[j])
            os_ = _mixer_b_latent(hs, kc, vc, b_w_qkv[j], b_rpb[j], b_w_o[j])
            new_state += [k_new, v_new]
        else:
            op = _mixer_c(hp, c_w_in[j], c_v_g[j], c_w_s[j], c_b_s[j], c_w_out[j])
            os_ = _mixer_c(hs, c_w_in[j], c_v_g[j], c_w_s[j], c_b_s[j], c_w_out[j])
        xp = xp + mp[2] * op
        xs = xs + ms[2] * os_
        hp = _modulate(xp, norm2_g[l], mp[3], mp[4])
        hs = _modulate(xs, norm2_g[l], ms[3], ms[4])
        xp = xp + mp[5] * _expert_choice_moe(hp, w_router[l], w_gate[l], w_up[l], w_down[l])
        xs = xs + ms[5] * _expert_choice_moe(hs, w_router[l], w_gate[l], w_up[l], w_down[l])
    y_prompt = _rmsnorm(xp, final_g)
    y_sample = _rmsnorm(xs, final_g)
    return (y_prompt, y_sample, *new_state)
```

```python
import functools

import numpy as np
import jax
import jax.numpy as jnp
from jax import lax
from jax.experimental import pallas as pl
from jax.experimental.pallas import tpu as pltpu

F32 = jnp.float32
BF16 = jnp.bfloat16

D_MODEL = 2048
HEAD_DIM = 128
N_HEADS = D_MODEL // HEAD_DIM
N_KV_HEADS = 4
KV_REP = N_HEADS // N_KV_HEADS
ATTN_SCALE = HEAD_DIM ** -0.5
SEQ = 256
DEC_SEQ = 1024
WINDOW = 128
WIN_SPAN = 3 * WINDOW
GRID_W = 64
GRID_ROWS = DEC_SEQ // GRID_W
NA_ROWS = 8
NA_COLS = 16
ROPE_BASE = 10000.0
ROPE_FREQS = HEAD_DIM // 4
CHUNK = 128
SG_GROUPS = 16
N_EXPERTS = 16
EC_FACTOR = 2
NORM_EPS = 1e-6
NEG_INF = -1e30
COND_ROWS = 16
LANES = 128
SOFTMAX_ROWS = 64
UNSELECTED = -4096.0
VMEM_LIMIT = 56 * 1024 * 1024


def _params(*sem):
    return pltpu.CompilerParams(dimension_semantics=sem, vmem_limit_bytes=VMEM_LIMIT)


def _dot(a, b):
    return jnp.dot(a, b, preferred_element_type=F32)


def _dot_nt(a, b):
    return lax.dot_general(a, b, (((1,), (1,)), ((), ())), preferred_element_type=F32)


def _silu(x):
    return x / (1.0 + jnp.exp(-x))


def _cond_of_tile(i, tm, n_prompt_rows):
    n_p = n_prompt_rows // tm
    per_b = DEC_SEQ // tm
    return jnp.where(i < n_p, 0, 1 + (i - n_p) // per_b)


def _adaln_kernel(c_ref, w_ref, b_ref, o_ref):
    s = _silu(c_ref[...]).astype(BF16)
    o_ref[0] = _dot(s, w_ref[0].astype(BF16)) + b_ref[0]


def _adaln_all(cond, w_mod, b_mod):
    depth, d, n = w_mod.shape
    tn = 1024
    return pl.pallas_call(
        _adaln_kernel,
        grid=(depth, n // tn),
        in_specs=[pl.BlockSpec((COND_ROWS, d), lambda l, j: (0, 0)),
                  pl.BlockSpec((1, d, tn), lambda l, j: (l, 0, j)),
                  pl.BlockSpec((1, 1, tn), lambda l, j: (l, 0, j))],
        out_specs=pl.BlockSpec((1, COND_ROWS, tn), lambda l, j: (l, 0, j)),
        out_shape=jax.ShapeDtypeStruct((depth, COND_ROWS, n), F32),
        compiler_params=_params("arbitrary", "arbitrary"),
        name="adaln",
    )(cond, w_mod, b_mod.reshape(depth, 1, n))


def _mod_spec(layer, which, tm, n_prompt_rows, row_axis, tn=D_MODEL, col_axis=None):
    def index(*ids):
        cond = _cond_of_tile(ids[row_axis], tm, n_prompt_rows)
        col = 0 if col_axis is None else ids[col_axis]
        return ((layer * COND_ROWS + cond) * 6 + which, 0, col)
    return pl.BlockSpec((1, 1, tn), index)


def _modulate_kernel(x_ref, g_ref, sh_ref, sc_ref, o_ref):
    x = x_ref[...]
    y = x * lax.rsqrt(jnp.mean(x * x, axis=-1, keepdims=True) + NORM_EPS) * g_ref[0]
    o_ref[...] = (y * (1.0 + sc_ref[0]) + sh_ref[0]).astype(o_ref.dtype)


def _modulate(x, g_all, mod, layer, k_shift, k_scale, n_prompt_rows):
    n, d = x.shape
    tm = 512
    return pl.pallas_call(
        _modulate_kernel,
        grid=(n // tm,),
        in_specs=[pl.BlockSpec((tm, d), lambda i: (i, 0)),
                  pl.BlockSpec((1, 1, d), lambda i: (layer, 0, 0)),
                  _mod_spec(layer, k_shift, tm, n_prompt_rows, 0),
                  _mod_spec(layer, k_scale, tm, n_prompt_rows, 0)],
        out_specs=pl.BlockSpec((tm, d), lambda i: (i, 0)),
        out_shape=jax.ShapeDtypeStruct((n, d), BF16),
        compiler_params=_params("arbitrary"),
        name="modulate",
    )(x, g_all, mod, mod)


def _rmsnorm_kernel(x_ref, g_ref, o_ref):
    x = x_ref[...]
    o_ref[...] = x * lax.rsqrt(jnp.mean(x * x, axis=-1, keepdims=True) + NORM_EPS) * g_ref[...]


def _final_norm(x, g, row0, rows):
    d = x.shape[1]
    tm = 512
    off = row0 // tm
    return pl.pallas_call(
        _rmsnorm_kernel,
        grid=(rows // tm,),
        in_specs=[pl.BlockSpec((tm, d), lambda i: (i + off, 0)),
                  pl.BlockSpec((1, d), lambda i: (0, 0))],
        out_specs=pl.BlockSpec((tm, d), lambda i: (i, 0)),
        out_shape=jax.ShapeDtypeStruct((rows, d), F32),
        compiler_params=_params("arbitrary"),
        name="final_norm",
    )(x, g.reshape(1, d))


def _mm_kernel(x_ref, w_ref, o_ref, wb_ref, *, act):
    @pl.when(pl.program_id(1) == 0)
    def _():
        wb_ref[...] = w_ref[...].astype(BF16)

    acc = _dot(x_ref[...], wb_ref[...])
    if act == "gelu":
        acc = jax.nn.gelu(acc)
    o_ref[...] = acc.astype(o_ref.dtype)


def _matmul(x, w_all, widx, out_dtype, act=None, tm=1024, tn=512):
    m, k = x.shape
    n = w_all.shape[2]
    return pl.pallas_call(
        functools.partial(_mm_kernel, act=act),
        grid=(n // tn, m // tm),
        in_specs=[pl.BlockSpec((tm, k), lambda j, i: (i, 0)),
                  pl.BlockSpec((None, k, tn), lambda j, i: (widx, 0, j))],
        out_specs=pl.BlockSpec((tm, tn), lambda j, i: (i, j)),
        out_shape=jax.ShapeDtypeStruct((m, n), out_dtype),
        scratch_shapes=[pltpu.VMEM((k, tn), BF16)],
        compiler_params=_params("arbitrary", "arbitrary"),
        name="matmul",
    )(x, w_all)


def _mm_res_kernel(x_ref, w_ref, r_ref, g_ref, o_ref, wb_ref):
    @pl.when(pl.program_id(1) == 0)
    def _():
        wb_ref[...] = w_ref[...].astype(BF16)

    o_ref[...] = r_ref[...] + g_ref[0] * _dot(x_ref[...], wb_ref[...])


def _matmul_residual(x, w_all, widx, res, mod, layer, k_gate, n_prompt_rows, tm=1024, tn=512):
    m, k = x.shape
    n = w_all.shape[2]
    return pl.pallas_call(
        _mm_res_kernel,
        grid=(n // tn, m // tm),
        in_specs=[pl.BlockSpec((tm, k), lambda j, i: (i, 0)),
                  pl.BlockSpec((None, k, tn), lambda j, i: (widx, 0, j)),
                  pl.BlockSpec((tm, tn), lambda j, i: (i, j)),
                  _mod_spec(layer, k_gate, tm, n_prompt_rows, 1, tn=tn, col_axis=0)],
        out_specs=pl.BlockSpec((tm, tn), lambda j, i: (i, j)),
        out_shape=jax.ShapeDtypeStruct((m, n), F32),
        scratch_shapes=[pltpu.VMEM((k, tn), BF16)],
        input_output_aliases={2: 0},
        compiler_params=_params("arbitrary", "arbitrary"),
        name="matmul_residual",
    )(x, w_all, res, mod)


def _softmax_parts(parts, sink=None):
    m = parts[0].max(axis=-1, keepdims=True)
    for s in parts[1:]:
        m = jnp.maximum(m, s.max(axis=-1, keepdims=True))
    if sink is not None:
        m = jnp.maximum(m, sink)
    ps = [jnp.exp(s - m) for s in parts]
    den = ps[0].sum(axis=-1, keepdims=True)
    for p in ps[1:]:
        den = den + p.sum(axis=-1, keepdims=True)
    if sink is not None:
        den = den + jnp.exp(sink - m)
    return ps, 1.0 / den


def _attn_ctx_kernel(sink_ref, q_ref, k_ref, v_ref, o_ref, *, rep, use_sink):
    k_b = v_b = None
    for h in range(N_HEADS):
        g = h // rep
        if h % rep == 0:
            k_b = k_ref[:, g * HEAD_DIM:(g + 1) * HEAD_DIM].astype(BF16)
            v_b = v_ref[:, g * HEAD_DIM:(g + 1) * HEAD_DIM].astype(BF16)
        q = (q_ref[:, h * HEAD_DIM:(h + 1) * HEAD_DIM] * ATTN_SCALE).astype(BF16)
        s = _dot_nt(q, k_b)
        (p,), inv = _softmax_parts([s], sink_ref[h] if use_sink else None)
        o = _dot(p.astype(BF16), v_b) * inv
        o_ref[:, h * HEAD_DIM:(h + 1) * HEAD_DIM] = o.astype(o_ref.dtype)


def _attn_ctx(qkv, sink, n_rows, n_batch, n_kv):
    wq = N_HEADS * HEAD_DIM
    wk = n_kv * HEAD_DIM
    use_sink = sink is not None
    if sink is None:
        sink = jnp.zeros((N_HEADS,), F32)
    return pl.pallas_call(
        functools.partial(_attn_ctx_kernel, rep=N_HEADS // n_kv, use_sink=use_sink),
        grid=(n_batch,),
        in_specs=[pl.BlockSpec(memory_space=pltpu.SMEM),
                  pl.BlockSpec((SEQ, wq), lambda b: (b, 0)),
                  pl.BlockSpec((SEQ, wk), lambda b: (b, wq // wk)),
                  pl.BlockSpec((SEQ, wk), lambda b: (b, wq // wk + 1))],
        out_specs=pl.BlockSpec((SEQ, wq), lambda b: (b, 0)),
        out_shape=jax.ShapeDtypeStruct((n_rows, wq), BF16),
        compiler_params=_params("arbitrary"),
        name="attn_ctx",
    )(sink, qkv, qkv, qkv)


def _rope(x, cos_t, sin_t):
    lane = lax.broadcasted_iota(jnp.int32, x.shape, 1)
    partner = jnp.where((lane & ROPE_FREQS) == 0,
                        pltpu.roll(x, HEAD_DIM - ROPE_FREQS, 1), pltpu.roll(x, ROPE_FREQS, 1))
    return x * cos_t + partner * sin_t


def _attn_win_kernel(sink_ref, q_ref, k_ref, v_ref, kc_ref, vc_ref, cq_ref, sq_ref, ck_ref, sk_ref,
                     _, o_ref, kr_ref):
    i = pl.program_id(1)

    @pl.when(i == 0)
    def _():
        for g in range(N_KV_HEADS):
            cols = slice(g * HEAD_DIM, (g + 1) * HEAD_DIM)
            kr_ref[:, cols] = _rope(k_ref[:, cols], ck_ref[...], sk_ref[...]).astype(BF16)

    start = pl.multiple_of(jnp.clip(i * WINDOW - WINDOW, 0, DEC_SEQ - WIN_SPAN), WINDOW)
    qpos = i * WINDOW + lax.broadcasted_iota(jnp.int32, (WINDOW, WIN_SPAN), 0)
    kpos = start + lax.broadcasted_iota(jnp.int32, (WINDOW, WIN_SPAN), 1)
    band = jnp.where(jnp.abs(qpos - kpos) <= WINDOW, 0.0, NEG_INF)
    cq, sq = cq_ref[...], sq_ref[...]
    for g in range(N_KV_HEADS):
        cols = slice(g * HEAD_DIM, (g + 1) * HEAD_DIM)
        kw = kr_ref[pl.ds(start, WIN_SPAN), cols]
        vw = v_ref[pl.ds(start, WIN_SPAN), cols].astype(BF16)
        kc = kc_ref[0, :, cols].astype(BF16)
        vc = vc_ref[0, :, cols].astype(BF16)
        heads = range(g * KV_REP, (g + 1) * KV_REP)
        qs = jnp.concatenate(
            [_rope(q_ref[:, h * HEAD_DIM:(h + 1) * HEAD_DIM], cq, sq) * ATTN_SCALE for h in heads],
            axis=0).astype(BF16)
        s_w = _dot_nt(qs, kw)
        s_c = _dot_nt(qs, kc)
        pcs, pws, invs = [], [], []
        for t in range(KV_REP * WINDOW // SOFTMAX_ROWS):
            rows = slice(t * SOFTMAX_ROWS, (t + 1) * SOFTMAX_ROWS)
            q0 = (t * SOFTMAX_ROWS) % WINDOW
            sink = sink_ref[g * KV_REP + t * SOFTMAX_ROWS // WINDOW]
            (p_c, p_w), inv = _softmax_parts([s_c[rows], s_w[rows] + band[q0:q0 + SOFTMAX_ROWS]], sink)
            pcs.append(p_c.astype(BF16))
            pws.append(p_w.astype(BF16))
            invs.append(inv)
        o = (_dot(jnp.concatenate(pcs, axis=0), vc) + _dot(jnp.concatenate(pws, axis=0), vw)
             ) * jnp.concatenate(invs, axis=0)
        for r, h in enumerate(heads):
            o_ref[:, h * HEAD_DIM:(h + 1) * HEAD_DIM] = o[r * WINDOW:(r + 1) * WINDOW].astype(o_ref.dtype)


def _rope_tables(n):
    t = np.arange(n)
    row = jnp.asarray(t // GRID_W, F32)
    col = jnp.asarray(t % GRID_W, F32)
    inv = ROPE_BASE ** (-jnp.arange(ROPE_FREQS, dtype=F32) / ROPE_FREQS)
    ar, ac = row[:, None] * inv, col[:, None] * inv
    cos_t = jnp.concatenate([jnp.cos(ar), jnp.cos(ar), jnp.cos(ac), jnp.cos(ac)], axis=1)
    sin_t = jnp.concatenate([-jnp.sin(ar), jnp.sin(ar), -jnp.sin(ac), jnp.sin(ac)], axis=1)
    return cos_t, sin_t


def _attn_win(qkv, kc, vc, sink, o_buf, n_prompt_rows, n_batch):
    wq = N_HEADS * HEAD_DIM
    wk = N_KV_HEADS * HEAD_DIM
    cos_t, sin_t = _rope_tables(DEC_SEQ)
    nqb = DEC_SEQ // WINDOW
    qb0 = n_prompt_rows // WINDOW
    sb0 = n_prompt_rows // DEC_SEQ
    kc = kc.reshape(n_batch, -1, wk)
    vc = vc.reshape(n_batch, -1, wk)
    past = kc.shape[1]
    return pl.pallas_call(
        _attn_win_kernel,
        grid=(n_batch, nqb),
        in_specs=[pl.BlockSpec(memory_space=pltpu.SMEM),
                  pl.BlockSpec((WINDOW, wq), lambda b, i: (qb0 + b * nqb + i, 0)),
                  pl.BlockSpec((DEC_SEQ, wk), lambda b, i: (sb0 + b, wq // wk)),
                  pl.BlockSpec((DEC_SEQ, wk), lambda b, i: (sb0 + b, wq // wk + 1)),
                  pl.BlockSpec((1, past, wk), lambda b, i: (b, 0, 0)),
                  pl.BlockSpec((1, past, wk), lambda b, i: (b, 0, 0)),
                  pl.BlockSpec((WINDOW, HEAD_DIM), lambda b, i: (i, 0)),
                  pl.BlockSpec((WINDOW, HEAD_DIM), lambda b, i: (i, 0)),
                  pl.BlockSpec((DEC_SEQ, HEAD_DIM), lambda b, i: (0, 0)),
                  pl.BlockSpec((DEC_SEQ, HEAD_DIM), lambda b, i: (0, 0)),
                  pl.BlockSpec(memory_space=pl.ANY)],
        out_specs=pl.BlockSpec((WINDOW, wq), lambda b, i: (qb0 + b * nqb + i, 0)),
        out_shape=jax.ShapeDtypeStruct(o_buf.shape, o_buf.dtype),
        scratch_shapes=[pltpu.VMEM((DEC_SEQ, wk), BF16)],
        input_output_aliases={10: 0},
        compiler_params=_params("arbitrary", "arbitrary"),
        name="attn_win",
    )(sink, qkv, qkv, qkv, kc, vc, cos_t, sin_t, cos_t, sin_t, o_buf)


def _na_row_start(r):
    return min(max(r - NA_ROWS // 2, 0), GRID_ROWS - NA_ROWS)


def _attn_na_kernel(q_ref, k_ref, v_ref, kc_ref, vc_ref, bias_ref, _, o_ref):
    q = (q_ref[...] * ATTN_SCALE).astype(BF16)
    k = k_ref[...].astype(BF16)
    v = v_ref[...].astype(BF16)
    kc = kc_ref[0].astype(BF16)
    vc = vc_ref[0].astype(BF16)
    s_ctx = _dot_nt(q, kc)
    span = NA_ROWS * GRID_W
    rows = [slice(r * GRID_W, (r + 1) * GRID_W) for r in range(GRID_ROWS)]
    keys = [slice(_na_row_start(r) * GRID_W, _na_row_start(r) * GRID_W + span) for r in range(GRID_ROWS)]
    s_nbr = [_dot_nt(q[rows[r]], k[keys[r]]) + bias_ref[0, _na_row_start(r) - r + NA_ROWS - 1]
             for r in range(GRID_ROWS)]
    probs = []
    for r in range(GRID_ROWS):
        (p_c, p_n), inv = _softmax_parts([s_ctx[rows[r]], s_nbr[r]])
        probs.append((p_c.astype(BF16), p_n.astype(BF16), inv))
    for r, (p_c, p_n, inv) in enumerate(probs):
        o = (_dot(p_c, vc) + _dot(p_n, v[keys[r]])) * inv
        o_ref[rows[r], :] = o.astype(o_ref.dtype)


def _na_bias_table(rpb):
    cq = np.arange(GRID_W)[:, None]
    ck = np.arange(GRID_W)[None, :]
    cs = np.clip(cq - NA_COLS // 2, 0, GRID_W - NA_COLS)
    colmask = (ck >= cs) & (ck < cs + NA_COLS)
    dc = np.clip(ck - cq + NA_COLS - 1, 0, 2 * NA_COLS - 2)
    full = jnp.where(colmask, rpb[:, :, dc].astype(F32), NEG_INF)
    tabs = [jnp.transpose(full[:, off:off + NA_ROWS], (0, 2, 1, 3)).reshape(N_HEADS, GRID_W, NA_ROWS * GRID_W)
            for off in range(NA_ROWS)]
    return jnp.stack(tabs, axis=1)


def _attn_na(qkv, kc, vc, rpb, o_buf, n_prompt_rows, n_batch):
    sb0 = n_prompt_rows // DEC_SEQ
    kc = kc.reshape(n_batch, -1, N_HEADS * HEAD_DIM)
    vc = vc.reshape(n_batch, -1, N_HEADS * HEAD_DIM)
    past = kc.shape[1]
    bias = _na_bias_table(rpb)
    return pl.pallas_call(
        _attn_na_kernel,
        grid=(N_HEADS, n_batch),
        in_specs=[pl.BlockSpec((DEC_SEQ, HEAD_DIM), lambda h, b: (sb0 + b, h)),
                  pl.BlockSpec((DEC_SEQ, HEAD_DIM), lambda h, b: (sb0 + b, N_HEADS + h)),
                  pl.BlockSpec((DEC_SEQ, HEAD_DIM), lambda h, b: (sb0 + b, 2 * N_HEADS + h)),
                  pl.BlockSpec((1, past, HEAD_DIM), lambda h, b: (b, 0, h)),
                  pl.BlockSpec((1, past, HEAD_DIM), lambda h, b: (b, 0, h)),
                  pl.BlockSpec((1, NA_ROWS, GRID_W, NA_ROWS * GRID_W), lambda h, b: (h, 0, 0, 0)),
                  pl.BlockSpec(memory_space=pl.ANY)],
        out_specs=pl.BlockSpec((DEC_SEQ, HEAD_DIM), lambda h, b: (sb0 + b, h)),
        out_shape=jax.ShapeDtypeStruct(o_buf.shape, o_buf.dtype),
        input_output_aliases={6: 0},
        compiler_params=_params("arbitrary", "arbitrary"),
        name="attn_na",
    )(qkv, qkv, qkv, kc, vc, bias, o_buf)


def _sgu_kernel(z_ref, vg_ref, ws_ref, bs_ref, o_ref):
    sg = z_ref.shape[1] // 2
    v = z_ref[:, sg:]
    vn = v * lax.rsqrt(jnp.mean(v * v, axis=-1, keepdims=True) + NORM_EPS) * vg_ref[...]
    for g in range(SG_GROUPS):
        cols = slice(g * CHUNK, (g + 1) * CHUNK)
        f = _dot(ws_ref[g].astype(BF16), vn[:, cols].astype(BF16)) + bs_ref[:, g:g + 1]
        o_ref[:, cols] = (z_ref[:, cols] * f).astype(o_ref.dtype)


def _sgu(z, v_g, w_s, b_s):
    n, two_sg = z.shape
    sg = two_sg // 2
    return pl.pallas_call(
        _sgu_kernel,
        grid=(n // CHUNK,),
        in_specs=[pl.BlockSpec((CHUNK, two_sg), lambda i: (i, 0)),
                  pl.BlockSpec((1, sg), lambda i: (0, 0)),
                  pl.BlockSpec((SG_GROUPS, CHUNK, CHUNK), lambda i: (0, 0, 0)),
                  pl.BlockSpec((CHUNK, SG_GROUPS), lambda i: (0, 0))],
        out_specs=pl.BlockSpec((CHUNK, sg), lambda i: (i, 0)),
        out_shape=jax.ShapeDtypeStruct((n, sg), BF16),
        compiler_params=_params("arbitrary"),
        name="sgu",
    )(z, v_g.reshape(1, sg), w_s, b_s.T)


def _route_kernel(h_ref, wr_ref, *rest, seq, cap, aliased):
    if aliased:
        rest = rest[2:]
    xe_ref, gate_ref, pt_ref = rest
    h = h_ref[...]
    w = wr_ref[...]
    w_hi = w.astype(BF16)
    w_lo = (w - w_hi.astype(F32)).astype(BF16)
    logits = _dot(h, w_hi) + _dot(h, w_lo)
    lane = lax.broadcasted_iota(jnp.int32, (seq, LANES), 1)
    logits = jnp.where(lane < N_EXPERTS, logits, NEG_INF)
    ex = jnp.exp(logits - logits.max(axis=-1, keepdims=True))
    aff = ex / ex.sum(axis=-1, keepdims=True)

    def refine(i, ans):
        cand = ans | jnp.left_shift(jnp.int32(1), 30 - i)
        cnt = jnp.sum(jnp.where(aff >= pltpu.bitcast(cand, F32), 1.0, 0.0), axis=0, keepdims=True)
        return jnp.where(cnt >= cap, cand, ans)

    thr = lax.fori_loop(0, 31, refine, jnp.zeros((1, LANES), jnp.int32))
    ge = aff >= pltpu.bitcast(thr, F32)
    gt = aff >= pltpu.bitcast(thr + 1, F32)
    gt_f = jnp.where(gt, 1.0, 0.0)
    eq_f = jnp.where(ge, 1.0, 0.0) - gt_f
    need = cap - jnp.sum(gt_f, axis=0, keepdims=True)
    tri = jnp.where(lax.broadcasted_iota(jnp.int32, (seq, seq), 1)
                    < lax.broadcasted_iota(jnp.int32, (seq, seq), 0), 1.0, 0.0).astype(BF16)
    eq_before = _dot(tri, eq_f.astype(BF16))
    sel = gt | (ge & (eq_before < need))
    slot = _dot(tri, jnp.where(sel, 1.0, 0.0).astype(BF16))
    tgt = jnp.where(sel, slot, UNSELECTED)
    tgt_t = tgt.T
    aff_t = aff.T

    sub = lax.broadcasted_iota(jnp.int32, (cap, seq), 0).astype(F32)
    group = max(1, 256 // cap)
    for e0 in range(0, N_EXPERTS, group):
        hits = [sub == tgt_t[e:e + 1, :] for e in range(e0, e0 + group)]
        for k, hit in enumerate(hits):
            gate_ref[e0 + k] = jnp.sum(jnp.where(hit, aff_t[e0 + k:e0 + k + 1, :], 0.0), axis=1, keepdims=True)
        onehot = jnp.concatenate([jnp.where(hit, 1.0, 0.0) for hit in hits], axis=0).astype(BF16)
        rows = _dot(onehot, h).astype(BF16)
        for k in range(group):
            xe_ref[e0 + k] = rows[k * cap:(k + 1) * cap]

    lane_f = lane.astype(F32)
    for j in range(N_EXPERTS * cap // LANES):
        chunk = jnp.zeros((seq, LANES), F32)
        for e in range(j * LANES // cap, ((j + 1) * LANES - 1) // cap + 1):
            chunk = jnp.where(lane_f == tgt[:, e:e + 1] + float(e * cap - j * LANES), 1.0, chunk)
        pt_ref[0, :, j * LANES:(j + 1) * LANES] = chunk.astype(BF16)


def _route_gather(h, w_router_pad, seq, n_batch, row0, slot0, n_slots, prev=None):
    d = h.shape[1]
    cap = EC_FACTOR * seq // N_EXPERTS
    hb0 = row0 // seq
    sb0 = slot0 // cap
    in_specs = [pl.BlockSpec((seq, d), lambda b: (hb0 + b, 0)),
                pl.BlockSpec((d, LANES), lambda b: (0, 0))]
    args = [h, w_router_pad]
    aliases = {}
    if prev is not None:
        in_specs += [pl.BlockSpec(memory_space=pl.ANY), pl.BlockSpec(memory_space=pl.ANY)]
        args += list(prev)
        aliases = {2: 0, 3: 1}
    return pl.pallas_call(
        functools.partial(_route_kernel, seq=seq, cap=cap, aliased=prev is not None),
        grid=(n_batch,),
        in_specs=in_specs,
        out_specs=[pl.BlockSpec((N_EXPERTS, cap, d), lambda b: (0, sb0 + b, 0)),
                   pl.BlockSpec((N_EXPERTS, cap, 1), lambda b: (0, sb0 + b, 0)),
                   pl.BlockSpec((1, seq, N_EXPERTS * cap), lambda b: (b, 0, 0))],
        out_shape=[jax.ShapeDtypeStruct((N_EXPERTS, n_slots, d), BF16),
                   jax.ShapeDtypeStruct((N_EXPERTS, n_slots, 1), F32),
                   jax.ShapeDtypeStruct((n_batch, seq, N_EXPERTS * cap), BF16)],
        input_output_aliases=aliases,
        compiler_params=_params("arbitrary"),
        name="route_gather",
    )(*args)


FFN_ROWS = 256


def _ffn_up_kernel(x_ref, wg_ref, wu_ref, o_ref, wgb_ref, wub_ref):
    wgb_ref[...] = wg_ref[0].astype(BF16)
    wub_ref[...] = wu_ref[0].astype(BF16)
    for m in range(x_ref.shape[1] // FFN_ROWS):
        rows = slice(m * FFN_ROWS, (m + 1) * FFN_ROWS)
        x = x_ref[0, rows, :]
        o_ref[0, rows, :] = (_silu(_dot(x, wgb_ref[...])) * _dot(x, wub_ref[...])).astype(o_ref.dtype)


def _ffn_up(xe, w_gate, w_up, layer, tf=512):
    ne, r, d = xe.shape
    f = w_gate.shape[3]
    return pl.pallas_call(
        _ffn_up_kernel,
        grid=(ne, f // tf),
        in_specs=[pl.BlockSpec((1, r, d), lambda e, j: (e, 0, 0)),
                  pl.BlockSpec((None, 1, d, tf), lambda e, j: (layer, e, 0, j)),
                  pl.BlockSpec((None, 1, d, tf), lambda e, j: (layer, e, 0, j))],
        out_specs=pl.BlockSpec((1, r, tf), lambda e, j: (e, 0, j)),
        out_shape=jax.ShapeDtypeStruct((ne, r, f), BF16),
        scratch_shapes=[pltpu.VMEM((d, tf), BF16), pltpu.VMEM((d, tf), BF16)],
        compiler_params=_params("arbitrary", "arbitrary"),
        name="ffn_up",
    )(xe, w_gate, w_up)


def _ffn_down_kernel(h_ref, wd_ref, g_ref, o_ref, wdb_ref):
    wdb_ref[...] = wd_ref[0].astype(BF16)
    for m in range(h_ref.shape[1] // FFN_ROWS):
        rows = slice(m * FFN_ROWS, (m + 1) * FFN_ROWS)
        o_ref[0, rows, :] = (_dot(h_ref[0, rows, :], wdb_ref[...]) * g_ref[0, rows, :]).astype(o_ref.dtype)


def _ffn_down(hmid, w_down, gate, layer, tn=1024):
    ne, r, f = hmid.shape
    d = w_down.shape[3]
    return pl.pallas_call(
        _ffn_down_kernel,
        grid=(ne, d // tn),
        in_specs=[pl.BlockSpec((1, r, f), lambda e, j: (e, 0, 0)),
                  pl.BlockSpec((None, 1, f, tn), lambda e, j: (layer, e, 0, j)),
                  pl.BlockSpec((1, r, 1), lambda e, j: (e, 0, 0))],
        out_specs=pl.BlockSpec((1, r, tn), lambda e, j: (e, 0, j)),
        out_shape=jax.ShapeDtypeStruct((ne, r, d), BF16),
        scratch_shapes=[pltpu.VMEM((f, tn), BF16)],
        compiler_params=_params("arbitrary", "arbitrary"),
        name="ffn_down",
    )(hmid, w_down, gate)


def _combine_kernel(pt_ref, y_ref, x_ref, g_ref, o_ref):
    ne, cap, tn = y_ref.shape
    y = y_ref[...].reshape(ne * cap, tn)
    o_ref[...] = x_ref[...] + g_ref[0] * _dot(pt_ref[0], y)


def _combine(pt, y, x, mod, layer, seq, row0, slot0, n_prompt_rows, tn=512):
    n_batch = pt.shape[0]
    d = x.shape[1]
    cap = EC_FACTOR * seq // N_EXPERTS
    xb0 = row0 // seq
    sb0 = slot0 // cap
    return pl.pallas_call(
        _combine_kernel,
        grid=(n_batch, d // tn),
        in_specs=[pl.BlockSpec((1, seq, N_EXPERTS * cap), lambda b, j: (b, 0, 0)),
                  pl.BlockSpec((N_EXPERTS, cap, tn), lambda b, j: (0, sb0 + b, j)),
                  pl.BlockSpec((seq, tn), lambda b, j: (xb0 + b, j)),
                  pl.BlockSpec((1, 1, tn),
                               lambda b, j: ((layer * COND_ROWS
                                              + _cond_of_tile(xb0 + b, seq, n_prompt_rows)) * 6 + 5, 0, j))],
        out_specs=pl.BlockSpec((seq, tn), lambda b, j: (xb0 + b, j)),
        out_shape=jax.ShapeDtypeStruct(x.shape, x.dtype),
        input_output_aliases={2: 0},
        compiler_params=_params("arbitrary", "arbitrary"),
        name="moe_combine",
    )(pt, y, x, mod)


def _moe(h, x, mod, layer, w_router, w_gate, w_up, w_down, n_prompt, n_latent):
    n_prompt_rows = n_prompt * SEQ
    cap_p = EC_FACTOR * SEQ // N_EXPERTS
    cap_s = EC_FACTOR * DEC_SEQ // N_EXPERTS
    slots_p = n_prompt * cap_p
    n_slots = slots_p + n_latent * cap_s
    wr = jnp.pad(w_router[layer], ((0, 0), (0, LANES - N_EXPERTS)))
    xe, gate, pt_p = _route_gather(h, wr, SEQ, n_prompt, 0, 0, n_slots)
    xe, gate, pt_s = _route_gather(h, wr, DEC_SEQ, n_latent, n_prompt_rows, slots_p, n_slots, prev=(xe, gate))
    y = _ffn_down(_ffn_up(xe, w_gate, w_up, layer), w_down, gate, layer)
    x = _combine(pt_p, y, x, mod, layer, SEQ, 0, 0, n_prompt_rows, tn=D_MODEL)
    return _combine(pt_s, y, x, mod, layer, DEC_SEQ, n_prompt_rows, slots_p, n_prompt_rows, tn=D_MODEL // 2)


def kernel(x_prompt, x_sample, cache_k_0, cache_v_0, cache_k_1, cache_v_1, cache_k_3, cache_v_3, c, c_ctx, norm1_g, norm2_g, w_mod, b_mod, a_w_qkv, a_sink, a_w_o, b_w_qkv, b_rpb, b_w_o, c_w_in, c_v_g, c_w_s, c_b_s, c_w_out, w_router, w_gate, w_up, w_down, final_g):
    n_prompt, _, d = x_prompt.shape
    n_latent = x_sample.shape[0]
    depth = w_mod.shape[0]
    p_rows = n_prompt * SEQ
    s_rows = n_latent * DEC_SEQ
    caches = {0: (cache_k_0, cache_v_0), 1: (cache_k_1, cache_v_1), 3: (cache_k_3, cache_v_3)}

    cond = jnp.concatenate([c_ctx[None], c, jnp.zeros((COND_ROWS - 1 - n_latent, d), F32)], axis=0)
    mod = _adaln_all(cond, w_mod, b_mod).reshape(depth * COND_ROWS * 6, 1, d)
    g1 = norm1_g.reshape(depth, 1, d)
    g2 = norm2_g.reshape(depth, 1, d)
    x = jnp.concatenate([x_prompt.reshape(p_rows, d), x_sample.reshape(s_rows, d)], axis=0)

    new_state = []
    for l in range(depth):
        kind, j = l % 3, l // 3
        h = _modulate(x, g1, mod, l, 0, 1, p_rows)
        if kind == 0:
            qkv = _matmul(h, a_w_qkv, j, F32)
            o = _attn_ctx(qkv, a_sink[j], p_rows + s_rows, n_prompt, N_KV_HEADS)
            o = _attn_win(qkv, *caches[l], a_sink[j], o, p_rows, n_latent)
            x = _matmul_residual(o, a_w_o, j, x, mod, l, 2, p_rows)
            wq, wk = N_HEADS * HEAD_DIM, N_KV_HEADS * HEAD_DIM
            new_state += [qkv[:p_rows, wq:wq + wk].reshape(n_prompt, SEQ, N_KV_HEADS, HEAD_DIM),
                          qkv[:p_rows, wq + wk:].reshape(n_prompt, SEQ, N_KV_HEADS, HEAD_DIM)]
        elif kind == 1:
            qkv = _matmul(h, b_w_qkv, j, F32)
            o = _attn_ctx(qkv, None, p_rows + s_rows, n_prompt, N_HEADS)
            o = _attn_na(qkv, *caches[l], b_rpb[j], o, p_rows, n_latent)
            x = _matmul_residual(o, b_w_o, j, x, mod, l, 2, p_rows)
            new_state += [qkv[:p_rows, d:2 * d].reshape(n_prompt, SEQ, N_HEADS, HEAD_DIM),
                          qkv[:p_rows, 2 * d:].reshape(n_prompt, SEQ, N_HEADS, HEAD_DIM)]
        else:
            z = _matmul(h, c_w_in, j, F32, act="gelu")
            m = _sgu(z, c_v_g[j], c_w_s[j], c_b_s[j])
            x = _matmul_residual(m, c_w_out, j, x, mod, l, 2, p_rows)
        h = _modulate(x, g2, mod, l, 3, 4, p_rows)
        x = _moe(h, x, mod, l, w_router, w_gate, w_up, w_down, n_prompt, n_latent)

    y_prompt = _final_norm(x, final_g, 0, p_rows).reshape(n_prompt, SEQ, d)
    y_sample = _final_norm(x, final_g, p_rows, s_rows).reshape(n_latent, DEC_SEQ, d)
    return (y_prompt, y_sample, *new_state)
```

```python
import functools

import numpy as np
import jax
import jax.numpy as jnp
from jax import lax
from jax.experimental import pallas as pl
from jax.experimental.pallas import tpu as pltpu

F32 = jnp.float32
BF16 = jnp.bfloat16

D_MODEL = 2048
HEAD_DIM = 128
N_HEADS = D_MODEL // HEAD_DIM
N_KV_HEADS = 4
KV_REP = N_HEADS // N_KV_HEADS
ATTN_SCALE = HEAD_DIM ** -0.5
SEQ = 256
DEC_SEQ = 1024
WINDOW = 128
WIN_SPAN = 3 * WINDOW
GRID_W = 64
GRID_ROWS = DEC_SEQ // GRID_W
NA_ROWS = 8
NA_COLS = 16
ROPE_BASE = 10000.0
ROPE_FREQS = HEAD_DIM // 4
CHUNK = 128
SG_GROUPS = 16
N_EXPERTS = 16
EC_FACTOR = 2
NORM_EPS = 1e-6
NEG_INF = -1e30
COND_ROWS = 16
LANES = 128
SOFTMAX_ROWS = 64
COUNT_SPLIT = 8
UNSELECTED = -4096.0
VMEM_LIMIT = 56 * 1024 * 1024


def _params(*sem):
    return pltpu.CompilerParams(dimension_semantics=sem, vmem_limit_bytes=VMEM_LIMIT)


def _dot(a, b):
    return jnp.dot(a, b, preferred_element_type=F32)


def _dot_nt(a, b):
    return lax.dot_general(a, b, (((1,), (1,)), ((), ())), preferred_element_type=F32)


def _silu(x):
    return x / (1.0 + jnp.exp(-x))


def _cond_of_tile(i, tm, n_prompt_rows):
    n_p = n_prompt_rows // tm
    per_b = DEC_SEQ // tm
    return jnp.where(i < n_p, 0, 1 + (i - n_p) // per_b)


def _adaln_kernel(c_ref, w_ref, b_ref, o_ref):
    s = _silu(c_ref[...]).astype(BF16)
    o_ref[0] = _dot(s, w_ref[0].astype(BF16)) + b_ref[0]


def _adaln_all(cond, w_mod, b_mod):
    depth, d, n = w_mod.shape
    tn = 1024
    return pl.pallas_call(
        _adaln_kernel,
        grid=(depth, n // tn),
        in_specs=[pl.BlockSpec((COND_ROWS, d), lambda l, j: (0, 0)),
                  pl.BlockSpec((1, d, tn), lambda l, j: (l, 0, j)),
                  pl.BlockSpec((1, 1, tn), lambda l, j: (l, 0, j))],
        out_specs=pl.BlockSpec((1, COND_ROWS, tn), lambda l, j: (l, 0, j)),
        out_shape=jax.ShapeDtypeStruct((depth, COND_ROWS, n), F32),
        compiler_params=_params("arbitrary", "arbitrary"),
        name="adaln",
    )(cond, w_mod, b_mod.reshape(depth, 1, n))


def _mod_spec(layer, which, tm, n_prompt_rows, row_axis, tn=D_MODEL, col_axis=None):
    def index(*ids):
        cond = _cond_of_tile(ids[row_axis], tm, n_prompt_rows)
        col = 0 if col_axis is None else ids[col_axis]
        return ((layer * COND_ROWS + cond) * 6 + which, 0, col)
    return pl.BlockSpec((1, 1, tn), index)


def _norm_modulate(x, ng, shift, scale):
    y = x * lax.rsqrt(jnp.mean(x * x, axis=-1, keepdims=True) + NORM_EPS) * ng
    return y * (1.0 + scale) + shift


def _modulate_kernel(x_ref, g_ref, sh_ref, sc_ref, o_ref):
    o_ref[...] = _norm_modulate(x_ref[...], g_ref[0], sh_ref[0], sc_ref[0]).astype(o_ref.dtype)


def _modulate(x, g_all, mod, layer, k_shift, k_scale, n_prompt_rows):
    n, d = x.shape
    tm = 512
    return pl.pallas_call(
        _modulate_kernel,
        grid=(n // tm,),
        in_specs=[pl.BlockSpec((tm, d), lambda i: (i, 0)),
                  pl.BlockSpec((1, 1, d), lambda i: (layer, 0, 0)),
                  _mod_spec(layer, k_shift, tm, n_prompt_rows, 0),
                  _mod_spec(layer, k_scale, tm, n_prompt_rows, 0)],
        out_specs=pl.BlockSpec((tm, d), lambda i: (i, 0)),
        out_shape=jax.ShapeDtypeStruct((n, d), BF16),
        compiler_params=_params("arbitrary"),
        name="modulate",
    )(x, g_all, mod, mod)


def _mm_kernel(x_ref, w_ref, o_ref, wb_ref, *, act):
    @pl.when(pl.program_id(1) == 0)
    def _():
        wb_ref[...] = w_ref[...].astype(BF16)

    acc = _dot(x_ref[...], wb_ref[...])
    if act == "gelu":
        acc = jax.nn.gelu(acc)
    o_ref[...] = acc.astype(o_ref.dtype)


def _matmul(x, w_all, widx, out_dtype, act=None, tm=512, tn=1024):
    m, k = x.shape
    n = w_all.shape[2]
    return pl.pallas_call(
        functools.partial(_mm_kernel, act=act),
        grid=(n // tn, m // tm),
        in_specs=[pl.BlockSpec((tm, k), lambda j, i: (i, 0)),
                  pl.BlockSpec((None, k, tn), lambda j, i: (widx, 0, j))],
        out_specs=pl.BlockSpec((tm, tn), lambda j, i: (i, j)),
        out_shape=jax.ShapeDtypeStruct((m, n), out_dtype),
        scratch_shapes=[pltpu.VMEM((k, tn), BF16)],
        compiler_params=_params("arbitrary", "arbitrary"),
        name="matmul",
    )(x, w_all)


def _mm_res_kernel(x_ref, w_ref, r_ref, g_ref, ng_ref, sh_ref, sc_ref, o_ref, h_ref):
    xn = r_ref[...] + g_ref[0] * _dot(x_ref[...], w_ref[...])
    o_ref[...] = xn
    h_ref[...] = _norm_modulate(xn, ng_ref[0], sh_ref[0], sc_ref[0]).astype(h_ref.dtype)


def _matmul_residual(x, w, res, mod, g2, layer, n_prompt_rows, tm=256):
    m, k = x.shape
    n = w.shape[1]
    return pl.pallas_call(
        _mm_res_kernel,
        grid=(m // tm,),
        in_specs=[pl.BlockSpec((tm, k), lambda i: (i, 0)),
                  pl.BlockSpec((k, n), lambda i: (0, 0)),
                  pl.BlockSpec((tm, n), lambda i: (i, 0)),
                  _mod_spec(layer, 2, tm, n_prompt_rows, 0),
                  pl.BlockSpec((1, 1, n), lambda i: (layer, 0, 0)),
                  _mod_spec(layer, 3, tm, n_prompt_rows, 0),
                  _mod_spec(layer, 4, tm, n_prompt_rows, 0)],
        out_specs=[pl.BlockSpec((tm, n), lambda i: (i, 0)),
                   pl.BlockSpec((tm, n), lambda i: (i, 0))],
        out_shape=[jax.ShapeDtypeStruct((m, n), F32), jax.ShapeDtypeStruct((m, n), BF16)],
        input_output_aliases={2: 0},
        compiler_params=_params("arbitrary"),
        name="matmul_residual",
    )(x, w, res, mod, g2, mod, mod)


def _softmax_parts(parts, sink=None):
    m = parts[0].max(axis=-1, keepdims=True)
    for s in parts[1:]:
        m = jnp.maximum(m, s.max(axis=-1, keepdims=True))
    if sink is not None:
        m = jnp.maximum(m, sink)
    ps = [jnp.exp(s - m) for s in parts]
    den = ps[0].sum(axis=-1, keepdims=True)
    for p in ps[1:]:
        den = den + p.sum(axis=-1, keepdims=True)
    if sink is not None:
        den = den + jnp.exp(sink - m)
    return ps, 1.0 / den


def _attn_ctx_kernel(sink_ref, q_ref, k_ref, v_ref, o_ref, *, rep, use_sink):
    k_b = v_b = None
    for h in range(N_HEADS):
        g = h // rep
        if h % rep == 0:
            k_b = k_ref[:, g * HEAD_DIM:(g + 1) * HEAD_DIM].astype(BF16)
            v_b = v_ref[:, g * HEAD_DIM:(g + 1) * HEAD_DIM].astype(BF16)
        q = (q_ref[:, h * HEAD_DIM:(h + 1) * HEAD_DIM] * ATTN_SCALE).astype(BF16)
        s = _dot_nt(q, k_b)
        (p,), inv = _softmax_parts([s], sink_ref[h] if use_sink else None)
        o = _dot(p.astype(BF16), v_b) * inv
        o_ref[:, h * HEAD_DIM:(h + 1) * HEAD_DIM] = o.astype(o_ref.dtype)


def _attn_ctx(qkv, sink, n_rows, n_batch, n_kv):
    wq = N_HEADS * HEAD_DIM
    wk = n_kv * HEAD_DIM
    use_sink = sink is not None
    if sink is None:
        sink = jnp.zeros((N_HEADS,), F32)
    return pl.pallas_call(
        functools.partial(_attn_ctx_kernel, rep=N_HEADS // n_kv, use_sink=use_sink),
        grid=(n_batch,),
        in_specs=[pl.BlockSpec(memory_space=pltpu.SMEM),
                  pl.BlockSpec((SEQ, wq), lambda b: (b, 0)),
                  pl.BlockSpec((SEQ, wk), lambda b: (b, wq // wk)),
                  pl.BlockSpec((SEQ, wk), lambda b: (b, wq // wk + 1))],
        out_specs=pl.BlockSpec((SEQ, wq), lambda b: (b, 0)),
        out_shape=jax.ShapeDtypeStruct((n_rows, wq), BF16),
        compiler_params=_params("arbitrary"),
        name="attn_ctx",
    )(sink, qkv, qkv, qkv)


def _rope(x, cos_t, sin_t):
    lane = lax.broadcasted_iota(jnp.int32, x.shape, 1)
    partner = jnp.where((lane & ROPE_FREQS) == 0,
                        pltpu.roll(x, HEAD_DIM - ROPE_FREQS, 1), pltpu.roll(x, ROPE_FREQS, 1))
    return x * cos_t + partner * sin_t


def _attn_win_kernel(sink_ref, q_ref, k_ref, v_ref, kc_ref, vc_ref, cq_ref, sq_ref, ck_ref, sk_ref,
                     _, o_ref, kr_ref):
    i = pl.program_id(1)

    @pl.when(i == 0)
    def _():
        for g in range(N_KV_HEADS):
            cols = slice(g * HEAD_DIM, (g + 1) * HEAD_DIM)
            kr_ref[:, cols] = _rope(k_ref[:, cols], ck_ref[...], sk_ref[...]).astype(BF16)

    start = pl.multiple_of(jnp.clip(i * WINDOW - WINDOW, 0, DEC_SEQ - WIN_SPAN), WINDOW)
    qpos = i * WINDOW + lax.broadcasted_iota(jnp.int32, (WINDOW, WIN_SPAN), 0)
    kpos = start + lax.broadcasted_iota(jnp.int32, (WINDOW, WIN_SPAN), 1)
    band = jnp.where(jnp.abs(qpos - kpos) <= WINDOW, 0.0, NEG_INF)
    cq, sq = cq_ref[...], sq_ref[...]
    for g in range(N_KV_HEADS):
        cols = slice(g * HEAD_DIM, (g + 1) * HEAD_DIM)
        kw = kr_ref[pl.ds(start, WIN_SPAN), cols]
        vw = v_ref[pl.ds(start, WIN_SPAN), cols].astype(BF16)
        kc = kc_ref[0, :, cols].astype(BF16)
        vc = vc_ref[0, :, cols].astype(BF16)
        heads = range(g * KV_REP, (g + 1) * KV_REP)
        qs = jnp.concatenate(
            [_rope(q_ref[:, h * HEAD_DIM:(h + 1) * HEAD_DIM], cq, sq) * ATTN_SCALE for h in heads],
            axis=0).astype(BF16)
        s_w = _dot_nt(qs, kw)
        s_c = _dot_nt(qs, kc)
        pcs, pws, invs = [], [], []
        for t in range(KV_REP * WINDOW // SOFTMAX_ROWS):
            rows = slice(t * SOFTMAX_ROWS, (t + 1) * SOFTMAX_ROWS)
            q0 = (t * SOFTMAX_ROWS) % WINDOW
            sink = sink_ref[g * KV_REP + t * SOFTMAX_ROWS // WINDOW]
            (p_c, p_w), inv = _softmax_parts([s_c[rows], s_w[rows] + band[q0:q0 + SOFTMAX_ROWS]], sink)
            pcs.append(p_c.astype(BF16))
            pws.append(p_w.astype(BF16))
            invs.append(inv)
        o = (_dot(jnp.concatenate(pcs, axis=0), vc) + _dot(jnp.concatenate(pws, axis=0), vw)
             ) * jnp.concatenate(invs, axis=0)
        for r, h in enumerate(heads):
            o_ref[:, h * HEAD_DIM:(h + 1) * HEAD_DIM] = o[r * WINDOW:(r + 1) * WINDOW].astype(o_ref.dtype)


def _rope_tables(n):
    t = np.arange(n)
    row = jnp.asarray(t // GRID_W, F32)
    col = jnp.asarray(t % GRID_W, F32)
    inv = ROPE_BASE ** (-jnp.arange(ROPE_FREQS, dtype=F32) / ROPE_FREQS)
    ar, ac = row[:, None] * inv, col[:, None] * inv
    cos_t = jnp.concatenate([jnp.cos(ar), jnp.cos(ar), jnp.cos(ac), jnp.cos(ac)], axis=1)
    sin_t = jnp.concatenate([-jnp.sin(ar), jnp.sin(ar), -jnp.sin(ac), jnp.sin(ac)], axis=1)
    return cos_t, sin_t


def _attn_win(qkv, kc, vc, sink, o_buf, n_prompt_rows, n_batch):
    wq = N_HEADS * HEAD_DIM
    wk = N_KV_HEADS * HEAD_DIM
    cos_t, sin_t = _rope_tables(DEC_SEQ)
    nqb = DEC_SEQ // WINDOW
    qb0 = n_prompt_rows // WINDOW
    sb0 = n_prompt_rows // DEC_SEQ
    kc = kc.reshape(n_batch, -1, wk)
    vc = vc.reshape(n_batch, -1, wk)
    past = kc.shape[1]
    return pl.pallas_call(
        _attn_win_kernel,
        grid=(n_batch, nqb),
        in_specs=[pl.BlockSpec(memory_space=pltpu.SMEM),
                  pl.BlockSpec((WINDOW, wq), lambda b, i: (qb0 + b * nqb + i, 0)),
                  pl.BlockSpec((DEC_SEQ, wk), lambda b, i: (sb0 + b, wq // wk)),
                  pl.BlockSpec((DEC_SEQ, wk), lambda b, i: (sb0 + b, wq // wk + 1)),
                  pl.BlockSpec((1, past, wk), lambda b, i: (b, 0, 0)),
                  pl.BlockSpec((1, past, wk), lambda b, i: (b, 0, 0)),
                  pl.BlockSpec((WINDOW, HEAD_DIM), lambda b, i: (i, 0)),
                  pl.BlockSpec((WINDOW, HEAD_DIM), lambda b, i: (i, 0)),
                  pl.BlockSpec((DEC_SEQ, HEAD_DIM), lambda b, i: (0, 0)),
                  pl.BlockSpec((DEC_SEQ, HEAD_DIM), lambda b, i: (0, 0)),
                  pl.BlockSpec(memory_space=pl.ANY)],
        out_specs=pl.BlockSpec((WINDOW, wq), lambda b, i: (qb0 + b * nqb + i, 0)),
        out_shape=jax.ShapeDtypeStruct(o_buf.shape, o_buf.dtype),
        scratch_shapes=[pltpu.VMEM((DEC_SEQ, wk), BF16)],
        input_output_aliases={10: 0},
        compiler_params=_params("arbitrary", "arbitrary"),
        name="attn_win",
    )(sink, qkv, qkv, qkv, kc, vc, cos_t, sin_t, cos_t, sin_t, o_buf)


def _na_row_start(r):
    return min(max(r - NA_ROWS // 2, 0), GRID_ROWS - NA_ROWS)


def _attn_na_kernel(q_ref, k_ref, v_ref, kc_ref, vc_ref, bias_ref, _, o_ref):
    q = (q_ref[...] * ATTN_SCALE).astype(BF16)
    k = k_ref[...].astype(BF16)
    v = v_ref[...].astype(BF16)
    kc = kc_ref[0].astype(BF16)
    vc = vc_ref[0].astype(BF16)
    s_ctx = _dot_nt(q, kc)
    span = NA_ROWS * GRID_W
    rows = [slice(r * GRID_W, (r + 1) * GRID_W) for r in range(GRID_ROWS)]
    keys = [slice(_na_row_start(r) * GRID_W, _na_row_start(r) * GRID_W + span) for r in range(GRID_ROWS)]
    s_nbr = [_dot_nt(q[rows[r]], k[keys[r]]) + bias_ref[0, _na_row_start(r) - r + NA_ROWS - 1]
             for r in range(GRID_ROWS)]
    probs = []
    for r in range(GRID_ROWS):
        (p_c, p_n), inv = _softmax_parts([s_ctx[rows[r]], s_nbr[r]])
        probs.append((p_c.astype(BF16), p_n.astype(BF16), inv))
    for r, (p_c, p_n, inv) in enumerate(probs):
        o = (_dot(p_c, vc) + _dot(p_n, v[keys[r]])) * inv
        o_ref[rows[r], :] = o.astype(o_ref.dtype)


def _na_bias_table(rpb):
    cq = np.arange(GRID_W)[:, None]
    ck = np.arange(GRID_W)[None, :]
    cs = np.clip(cq - NA_COLS // 2, 0, GRID_W - NA_COLS)
    colmask = (ck >= cs) & (ck < cs + NA_COLS)
    dc = np.clip(ck - cq + NA_COLS - 1, 0, 2 * NA_COLS - 2)
    full = jnp.where(colmask, rpb[:, :, dc].astype(F32), NEG_INF)
    tabs = [jnp.transpose(full[:, off:off + NA_ROWS], (0, 2, 1, 3)).reshape(N_HEADS, GRID_W, NA_ROWS * GRID_W)
            for off in range(NA_ROWS)]
    return jnp.stack(tabs, axis=1)


def _attn_na(qkv, kc, vc, rpb, o_buf, n_prompt_rows, n_batch):
    sb0 = n_prompt_rows // DEC_SEQ
    kc = kc.reshape(n_batch, -1, N_HEADS * HEAD_DIM)
    vc = vc.reshape(n_batch, -1, N_HEADS * HEAD_DIM)
    past = kc.shape[1]
    bias = _na_bias_table(rpb)
    return pl.pallas_call(
        _attn_na_kernel,
        grid=(N_HEADS, n_batch),
        in_specs=[pl.BlockSpec((DEC_SEQ, HEAD_DIM), lambda h, b: (sb0 + b, h)),
                  pl.BlockSpec((DEC_SEQ, HEAD_DIM), lambda h, b: (sb0 + b, N_HEADS + h)),
                  pl.BlockSpec((DEC_SEQ, HEAD_DIM), lambda h, b: (sb0 + b, 2 * N_HEADS + h)),
                  pl.BlockSpec((1, past, HEAD_DIM), lambda h, b: (b, 0, h)),
                  pl.BlockSpec((1, past, HEAD_DIM), lambda h, b: (b, 0, h)),
                  pl.BlockSpec((1, NA_ROWS, GRID_W, NA_ROWS * GRID_W), lambda h, b: (h, 0, 0, 0)),
                  pl.BlockSpec(memory_space=pl.ANY)],
        out_specs=pl.BlockSpec((DEC_SEQ, HEAD_DIM), lambda h, b: (sb0 + b, h)),
        out_shape=jax.ShapeDtypeStruct(o_buf.shape, o_buf.dtype),
        input_output_aliases={6: 0},
        compiler_params=_params("arbitrary", "arbitrary"),
        name="attn_na",
    )(qkv, qkv, qkv, kc, vc, bias, o_buf)


def _sgu_kernel(z_ref, vg_ref, ws_ref, bs_ref, o_ref):
    sg = z_ref.shape[1] // 2
    v = z_ref[:, sg:]
    vn = v * lax.rsqrt(jnp.mean(v * v, axis=-1, keepdims=True) + NORM_EPS) * vg_ref[...]
    for g in range(SG_GROUPS):
        cols = slice(g * CHUNK, (g + 1) * CHUNK)
        f = _dot(ws_ref[g].astype(BF16), vn[:, cols].astype(BF16)) + bs_ref[:, g:g + 1]
        o_ref[:, cols] = (z_ref[:, cols] * f).astype(o_ref.dtype)


def _sgu(z, v_g, w_s, b_s):
    n, two_sg = z.shape
    sg = two_sg // 2
    return pl.pallas_call(
        _sgu_kernel,
        grid=(n // CHUNK,),
        in_specs=[pl.BlockSpec((CHUNK, two_sg), lambda i: (i, 0)),
                  pl.BlockSpec((1, sg), lambda i: (0, 0)),
                  pl.BlockSpec((SG_GROUPS, CHUNK, CHUNK), lambda i: (0, 0, 0)),
                  pl.BlockSpec((CHUNK, SG_GROUPS), lambda i: (0, 0))],
        out_specs=pl.BlockSpec((CHUNK, sg), lambda i: (i, 0)),
        out_shape=jax.ShapeDtypeStruct((n, sg), BF16),
        compiler_params=_params("arbitrary"),
        name="sgu",
    )(z, v_g.reshape(1, sg), w_s, b_s.T)


def _route_kernel(h_ref, wr_ref, *rest, seq, cap, aliased):
    if aliased:
        rest = rest[2:]
    xe_ref, gate_ref, pt_ref = rest
    h = h_ref[...]
    w = wr_ref[...]
    w_hi = w.astype(BF16)
    w_lo = (w - w_hi.astype(F32)).astype(BF16)
    logits = _dot(h, w_hi) + _dot(h, w_lo)
    lane = lax.broadcasted_iota(jnp.int32, (seq, LANES), 1)
    logits = jnp.where(lane < N_EXPERTS, logits, NEG_INF)
    ex = jnp.exp(logits - logits.max(axis=-1, keepdims=True))
    aff = ex / ex.sum(axis=-1, keepdims=True)

    def count(mask):
        ones = jnp.where(mask, 1.0, 0.0).reshape(COUNT_SPLIT, seq // COUNT_SPLIT, LANES)
        return jnp.sum(jnp.sum(ones, axis=0), axis=0, keepdims=True)

    def refine(i, ans):
        cand = ans | jnp.left_shift(jnp.int32(1), 30 - i)
        return jnp.where(count(aff >= pltpu.bitcast(cand, F32)) >= cap, cand, ans)

    thr = lax.fori_loop(0, 31, refine, jnp.zeros((1, LANES), jnp.int32))
    ge = aff >= pltpu.bitcast(thr, F32)
    gt = aff >= pltpu.bitcast(thr + 1, F32)
    gt_f = jnp.where(gt, 1.0, 0.0)
    eq_f = jnp.where(ge, 1.0, 0.0) - gt_f
    need = cap - count(gt)
    tri = jnp.where(lax.broadcasted_iota(jnp.int32, (seq, seq), 1)
                    < lax.broadcasted_iota(jnp.int32, (seq, seq), 0), 1.0, 0.0).astype(BF16)
    eq_before = _dot(tri, eq_f.astype(BF16))
    sel = gt | (ge & (eq_before < need))
    slot = _dot(tri, jnp.where(sel, 1.0, 0.0).astype(BF16))
    tgt = jnp.where(sel, slot, UNSELECTED)
    tgt_t = tgt.T
    aff_t = aff.T

    sub = lax.broadcasted_iota(jnp.int32, (cap, seq), 0).astype(F32)
    group = max(1, 256 // cap)
    for e0 in range(0, N_EXPERTS, group):
        hits = [sub == tgt_t[e:e + 1, :] for e in range(e0, e0 + group)]
        for k, hit in enumerate(hits):
            gate_ref[e0 + k] = jnp.sum(jnp.where(hit, aff_t[e0 + k:e0 + k + 1, :], 0.0), axis=1, keepdims=True)
        onehot = jnp.concatenate([jnp.where(hit, 1.0, 0.0) for hit in hits], axis=0).astype(BF16)
        rows = _dot(onehot, h).astype(BF16)
        for k in range(group):
            xe_ref[e0 + k] = rows[k * cap:(k + 1) * cap]

    lane_f = lane.astype(F32)
    for j in range(N_EXPERTS * cap // LANES):
        chunk = jnp.zeros((seq, LANES), F32)
        for e in range(j * LANES // cap, ((j + 1) * LANES - 1) // cap + 1):
            chunk = jnp.where(lane_f == tgt[:, e:e + 1] + float(e * cap - j * LANES), 1.0, chunk)
        pt_ref[0, :, j * LANES:(j + 1) * LANES] = chunk.astype(BF16)


def _route_gather(h, w_router_pad, seq, n_batch, row0, slot0, n_slots, prev=None):
    d = h.shape[1]
    cap = EC_FACTOR * seq // N_EXPERTS
    hb0 = row0 // seq
    sb0 = slot0 // cap
    in_specs = [pl.BlockSpec((seq, d), lambda b: (hb0 + b, 0)),
                pl.BlockSpec((d, LANES), lambda b: (0, 0))]
    args = [h, w_router_pad]
    aliases = {}
    if prev is not None:
        in_specs += [pl.BlockSpec(memory_space=pl.ANY), pl.BlockSpec(memory_space=pl.ANY)]
        args += list(prev)
        aliases = {2: 0, 3: 1}
    return pl.pallas_call(
        functools.partial(_route_kernel, seq=seq, cap=cap, aliased=prev is not None),
        grid=(n_batch,),
        in_specs=in_specs,
        out_specs=[pl.BlockSpec((N_EXPERTS, cap, d), lambda b: (0, sb0 + b, 0)),
                   pl.BlockSpec((N_EXPERTS, cap, 1), lambda b: (0, sb0 + b, 0)),
                   pl.BlockSpec((1, seq, N_EXPERTS * cap), lambda b: (b, 0, 0))],
        out_shape=[jax.ShapeDtypeStruct((N_EXPERTS, n_slots, d), BF16),
                   jax.ShapeDtypeStruct((N_EXPERTS, n_slots, 1), F32),
                   jax.ShapeDtypeStruct((n_batch, seq, N_EXPERTS * cap), BF16)],
        input_output_aliases=aliases,
        compiler_params=_params("arbitrary"),
        name="route_gather",
    )(*args)


FFN_ROWS = 256


def _ffn_up_kernel(x_ref, wg_ref, wu_ref, o_ref, wgb_ref, wub_ref):
    wgb_ref[...] = wg_ref[0].astype(BF16)
    wub_ref[...] = wu_ref[0].astype(BF16)
    for m in range(x_ref.shape[1] // FFN_ROWS):
        rows = slice(m * FFN_ROWS, (m + 1) * FFN_ROWS)
        x = x_ref[0, rows, :]
        o_ref[0, rows, :] = (_silu(_dot(x, wgb_ref[...])) * _dot(x, wub_ref[...])).astype(o_ref.dtype)


def _ffn_up(xe, w_gate, w_up, layer, tf=512):
    ne, r, d = xe.shape
    f = w_gate.shape[3]
    return pl.pallas_call(
        _ffn_up_kernel,
        grid=(ne, f // tf),
        in_specs=[pl.BlockSpec((1, r, d), lambda e, j: (e, 0, 0)),
                  pl.BlockSpec((None, 1, d, tf), lambda e, j: (layer, e, 0, j)),
                  pl.BlockSpec((None, 1, d, tf), lambda e, j: (layer, e, 0, j))],
        out_specs=pl.BlockSpec((1, r, tf), lambda e, j: (e, 0, j)),
        out_shape=jax.ShapeDtypeStruct((ne, r, f), BF16),
        scratch_shapes=[pltpu.VMEM((d, tf), BF16), pltpu.VMEM((d, tf), BF16)],
        compiler_params=_params("arbitrary", "arbitrary"),
        name="ffn_up",
    )(xe, w_gate, w_up)


def _ffn_down_kernel(h_ref, wd_ref, g_ref, o_ref, wdb_ref):
    wdb_ref[...] = wd_ref[0].astype(BF16)
    for m in range(h_ref.shape[1] // FFN_ROWS):
        rows = slice(m * FFN_ROWS, (m + 1) * FFN_ROWS)
        o_ref[0, rows, :] = (_dot(h_ref[0, rows, :], wdb_ref[...]) * g_ref[0, rows, :]).astype(o_ref.dtype)


def _ffn_down(hmid, w_down, gate, layer, tn=1024):
    ne, r, f = hmid.shape
    d = w_down.shape[3]
    return pl.pallas_call(
        _ffn_down_kernel,
        grid=(ne, d // tn),
        in_specs=[pl.BlockSpec((1, r, f), lambda e, j: (e, 0, 0)),
                  pl.BlockSpec((None, 1, f, tn), lambda e, j: (layer, e, 0, j)),
                  pl.BlockSpec((1, r, 1), lambda e, j: (e, 0, 0))],
        out_specs=pl.BlockSpec((1, r, tn), lambda e, j: (e, 0, j)),
        out_shape=jax.ShapeDtypeStruct((ne, r, d), BF16),
        scratch_shapes=[pltpu.VMEM((f, tn), BF16)],
        compiler_params=_params("arbitrary", "arbitrary"),
        name="ffn_down",
    )(hmid, w_down, gate)


COMBINE_ROWS = 256


def _moe_tokens(pt_ref, y_ref, x_ref, g_ref):
    ne, cap, d = y_ref.shape
    return x_ref[...] + g_ref[0] * _dot(pt_ref[0], y_ref[...].reshape(ne * cap, d))


def _combine_next_kernel(pt_ref, y_ref, x_ref, g_ref, ng_ref, sh_ref, sc_ref, *rest):
    o_ref, h_ref = rest[-2:]
    xn = _moe_tokens(pt_ref, y_ref, x_ref, g_ref)
    o_ref[...] = xn
    h_ref[...] = _norm_modulate(xn, ng_ref[0], sh_ref[0], sc_ref[0]).astype(h_ref.dtype)


def _combine_final_kernel(pt_ref, y_ref, x_ref, g_ref, ng_ref, o_ref):
    xn = _moe_tokens(pt_ref, y_ref, x_ref, g_ref)
    o_ref[...] = xn * lax.rsqrt(jnp.mean(xn * xn, axis=-1, keepdims=True) + NORM_EPS) * ng_ref[...]


def _combine(pt, y, x, mod, layer, seq, row0, slot0, n_prompt_rows, nxt, h_prev=None):
    n_batch = pt.shape[0]
    d = x.shape[1]
    cap = EC_FACTOR * seq // N_EXPERTS
    tr = COMBINE_ROWS
    per_seq = seq // tr
    xt0 = row0 // tr
    sb0 = slot0 // cap

    def tile(b, t):
        return xt0 + b * per_seq + t

    def mod_spec(lyr, which):
        return pl.BlockSpec(
            (1, 1, d), lambda b, t: ((lyr * COND_ROWS + _cond_of_tile(tile(b, t), tr, n_prompt_rows)) * 6 + which,
                                     0, 0))

    in_specs = [pl.BlockSpec((1, tr, N_EXPERTS * cap), lambda b, t: (b, t, 0)),
                pl.BlockSpec((N_EXPERTS, cap, d), lambda b, t: (0, sb0 + b, 0)),
                pl.BlockSpec((tr, d), lambda b, t: (tile(b, t), 0)),
                mod_spec(layer, 5)]
    if isinstance(nxt, tuple):
        g_next, l_next = nxt
        in_specs += [pl.BlockSpec((1, 1, d), lambda b, t: (l_next, 0, 0)), mod_spec(l_next, 0), mod_spec(l_next, 1)]
        args = [pt, y, x, mod, g_next, mod, mod]
        aliases = {2: 0}
        if h_prev is not None:
            in_specs.append(pl.BlockSpec(memory_space=pl.ANY))
            args.append(h_prev)
            aliases[7] = 1
        return pl.pallas_call(
            _combine_next_kernel,
            grid=(n_batch, per_seq),
            in_specs=in_specs,
            out_specs=[pl.BlockSpec((tr, d), lambda b, t: (tile(b, t), 0)),
                       pl.BlockSpec((tr, d), lambda b, t: (tile(b, t), 0))],
            out_shape=[jax.ShapeDtypeStruct(x.shape, x.dtype), jax.ShapeDtypeStruct(x.shape, BF16)],
            input_output_aliases=aliases,
            compiler_params=_params("arbitrary", "arbitrary"),
            name="moe_combine",
        )(*args)
    in_specs.append(pl.BlockSpec((1, d), lambda b, t: (0, 0)))
    return pl.pallas_call(
        _combine_final_kernel,
        grid=(n_batch, per_seq),
        in_specs=in_specs,
        out_specs=pl.BlockSpec((tr, d), lambda b, t: (b * per_seq + t, 0)),
        out_shape=jax.ShapeDtypeStruct((n_batch * seq, d), F32),
        compiler_params=_params("arbitrary", "arbitrary"),
        name="moe_combine_final",
    )(pt, y, x, mod, nxt)


def _moe(h, x, mod, layer, w_router, w_gate, w_up, w_down, n_prompt, n_latent, nxt):
    n_prompt_rows = n_prompt * SEQ
    cap_p = EC_FACTOR * SEQ // N_EXPERTS
    cap_s = EC_FACTOR * DEC_SEQ // N_EXPERTS
    slots_p = n_prompt * cap_p
    n_slots = slots_p + n_latent * cap_s
    wr = jnp.pad(w_router[layer], ((0, 0), (0, LANES - N_EXPERTS)))
    xe, gate, pt_p = _route_gather(h, wr, SEQ, n_prompt, 0, 0, n_slots)
    xe, gate, pt_s = _route_gather(h, wr, DEC_SEQ, n_latent, n_prompt_rows, slots_p, n_slots, prev=(xe, gate))
    y = _ffn_down(_ffn_up(xe, w_gate, w_up, layer), w_down, gate, layer)
    if isinstance(nxt, tuple):
        x, h = _combine(pt_p, y, x, mod, layer, SEQ, 0, 0, n_prompt_rows, nxt)
        return _combine(pt_s, y, x, mod, layer, DEC_SEQ, n_prompt_rows, slots_p, n_prompt_rows, nxt, h_prev=h)
    return (_combine(pt_p, y, x, mod, layer, SEQ, 0, 0, n_prompt_rows, nxt),
            _combine(pt_s, y, x, mod, layer, DEC_SEQ, n_prompt_rows, slots_p, n_prompt_rows, nxt))


def kernel(x_prompt, x_sample, cache_k_0, cache_v_0, cache_k_1, cache_v_1, cache_k_3, cache_v_3, c, c_ctx, norm1_g, norm2_g, w_mod, b_mod, a_w_qkv, a_sink, a_w_o, b_w_qkv, b_rpb, b_w_o, c_w_in, c_v_g, c_w_s, c_b_s, c_w_out, w_router, w_gate, w_up, w_down, final_g):
    n_prompt, _, d = x_prompt.shape
    n_latent = x_sample.shape[0]
    depth = w_mod.shape[0]
    p_rows = n_prompt * SEQ
    s_rows = n_latent * DEC_SEQ
    caches = {0: (cache_k_0, cache_v_0), 1: (cache_k_1, cache_v_1), 3: (cache_k_3, cache_v_3)}

    cond = jnp.concatenate([c_ctx[None], c, jnp.zeros((COND_ROWS - 1 - n_latent, d), F32)], axis=0)
    mod = _adaln_all(cond, w_mod, b_mod).reshape(depth * COND_ROWS * 6, 1, d)
    g1 = norm1_g.reshape(depth, 1, d)
    g2 = norm2_g.reshape(depth, 1, d)
    x = jnp.concatenate([x_prompt.reshape(p_rows, d), x_sample.reshape(s_rows, d)], axis=0)

    new_state = []
    h = _modulate(x, g1, mod, 0, 0, 1, p_rows)
    for l in range(depth):
        kind, j = l % 3, l // 3
        if kind == 0:
            qkv = _matmul(h, a_w_qkv, j, F32)
            o = _attn_ctx(qkv, a_sink[j], p_rows + s_rows, n_prompt, N_KV_HEADS)
            o = _attn_win(qkv, *caches[l], a_sink[j], o, p_rows, n_latent)
            w_out = a_w_o[j]
            wq, wk = N_HEADS * HEAD_DIM, N_KV_HEADS * HEAD_DIM
            new_state += [qkv[:p_rows, wq:wq + wk].reshape(n_prompt, SEQ, N_KV_HEADS, HEAD_DIM),
                          qkv[:p_rows, wq + wk:].reshape(n_prompt, SEQ, N_KV_HEADS, HEAD_DIM)]
        elif kind == 1:
            qkv = _matmul(h, b_w_qkv, j, F32)
            o = _attn_ctx(qkv, None, p_rows + s_rows, n_prompt, N_HEADS)
            o = _attn_na(qkv, *caches[l], b_rpb[j], o, p_rows, n_latent)
            w_out = b_w_o[j]
            new_state += [qkv[:p_rows, d:2 * d].reshape(n_prompt, SEQ, N_HEADS, HEAD_DIM),
                          qkv[:p_rows, 2 * d:].reshape(n_prompt, SEQ, N_HEADS, HEAD_DIM)]
        else:
            z = _matmul(h, c_w_in, j, F32, act="gelu")
            o = _sgu(z, c_v_g[j], c_w_s[j], c_b_s[j])
            w_out = c_w_out[j]
        x, h = _matmul_residual(o, w_out.astype(BF16), x, mod, g2, l, p_rows)
        if l + 1 < depth:
            x, h = _moe(h, x, mod, l, w_router, w_gate, w_up, w_down, n_prompt, n_latent, (g1, l + 1))
    y_prompt, y_sample = _moe(h, x, mod, depth - 1, w_router, w_gate, w_up, w_down, n_prompt, n_latent,
                              final_g.reshape(1, d))
    return (y_prompt.reshape(n_prompt, SEQ, d), y_sample.reshape(n_latent, DEC_SEQ, d), *new_state)
```

```python
import functools

import numpy as np
import jax
import jax.numpy as jnp
from jax import lax
from jax.experimental import pallas as pl
from jax.experimental.pallas import tpu as pltpu

F32 = jnp.float32
BF16 = jnp.bfloat16

D_MODEL = 2048
HEAD_DIM = 128
N_HEADS = D_MODEL // HEAD_DIM
N_KV_HEADS = 4
KV_REP = N_HEADS // N_KV_HEADS
ATTN_SCALE = HEAD_DIM ** -0.5
SEQ = 256
DEC_SEQ = 1024
WINDOW = 128
WIN_SPAN = 3 * WINDOW
GRID_W = 64
GRID_ROWS = DEC_SEQ // GRID_W
NA_ROWS = 8
NA_COLS = 16
ROPE_BASE = 10000.0
ROPE_FREQS = HEAD_DIM // 4
CHUNK = 128
SG_GROUPS = 16
N_EXPERTS = 16
EC_FACTOR = 2
NORM_EPS = 1e-6
NEG_INF = -1e30
COND_ROWS = 16
LANES = 128
SOFTMAX_ROWS = 64
MM_ROWS = 256
COUNT_SPLIT = 8
UNSELECTED = -4096.0
VMEM_LIMIT = 56 * 1024 * 1024


def _params(*sem):
    return pltpu.CompilerParams(dimension_semantics=sem, vmem_limit_bytes=VMEM_LIMIT)


def _dot(a, b):
    return jnp.dot(a, b, preferred_element_type=F32)


def _dot_nt(a, b):
    return lax.dot_general(a, b, (((1,), (1,)), ((), ())), preferred_element_type=F32)


def _silu(x):
    return x / (1.0 + jnp.exp(-x))


def _cond_of_tile(i, tm, n_prompt_rows):
    n_p = n_prompt_rows // tm
    per_b = DEC_SEQ // tm
    return jnp.where(i < n_p, 0, 1 + (i - n_p) // per_b)


def _adaln_kernel(c_ref, w_ref, b_ref, o_ref):
    s = _silu(c_ref[...]).astype(BF16)
    o_ref[0] = _dot(s, w_ref[0].astype(BF16)) + b_ref[0]


def _adaln_all(cond, w_mod, b_mod):
    depth, d, n = w_mod.shape
    tn = 1024
    return pl.pallas_call(
        _adaln_kernel,
        grid=(depth, n // tn),
        in_specs=[pl.BlockSpec((COND_ROWS, d), lambda l, j: (0, 0)),
                  pl.BlockSpec((1, d, tn), lambda l, j: (l, 0, j)),
                  pl.BlockSpec((1, 1, tn), lambda l, j: (l, 0, j))],
        out_specs=pl.BlockSpec((1, COND_ROWS, tn), lambda l, j: (l, 0, j)),
        out_shape=jax.ShapeDtypeStruct((depth, COND_ROWS, n), F32),
        compiler_params=_params("arbitrary", "arbitrary"),
        name="adaln",
    )(cond, w_mod, b_mod.reshape(depth, 1, n))


def _mod_spec(layer, which, tm, n_prompt_rows, row_axis, tn=D_MODEL, col_axis=None):
    def index(*ids):
        cond = _cond_of_tile(ids[row_axis], tm, n_prompt_rows)
        col = 0 if col_axis is None else ids[col_axis]
        return ((layer * COND_ROWS + cond) * 6 + which, 0, col)
    return pl.BlockSpec((1, 1, tn), index)


def _norm_modulate(x, ng, shift, scale):
    y = x * lax.rsqrt(jnp.mean(x * x, axis=-1, keepdims=True) + NORM_EPS) * ng
    return y * (1.0 + scale) + shift


def _modulate_kernel(x_ref, g_ref, sh_ref, sc_ref, o_ref):
    o_ref[...] = _norm_modulate(x_ref[...], g_ref[0], sh_ref[0], sc_ref[0]).astype(o_ref.dtype)


def _modulate(x, g_all, mod, layer, k_shift, k_scale, n_prompt_rows):
    n, d = x.shape
    tm = 512
    return pl.pallas_call(
        _modulate_kernel,
        grid=(n // tm,),
        in_specs=[pl.BlockSpec((tm, d), lambda i: (i, 0)),
                  pl.BlockSpec((1, 1, d), lambda i: (layer, 0, 0)),
                  _mod_spec(layer, k_shift, tm, n_prompt_rows, 0),
                  _mod_spec(layer, k_scale, tm, n_prompt_rows, 0)],
        out_specs=pl.BlockSpec((tm, d), lambda i: (i, 0)),
        out_shape=jax.ShapeDtypeStruct((n, d), BF16),
        compiler_params=_params("arbitrary"),
        name="modulate",
    )(x, g_all, mod, mod)


def _mm_kernel(x_ref, w_ref, o_ref, wb_ref, *, act):
    @pl.when(pl.program_id(1) == 0)
    def _():
        wb_ref[...] = w_ref[...].astype(BF16)

    for c in range(x_ref.shape[0] // MM_ROWS):
        rows = slice(c * MM_ROWS, (c + 1) * MM_ROWS)
        acc = _dot(x_ref[rows, :], wb_ref[...])
        if act == "gelu":
            acc = jax.nn.gelu(acc)
        o_ref[rows, :] = acc.astype(o_ref.dtype)


def _matmul(x, w_all, widx, out_dtype, act=None, tm=1024, tn=1024):
    m, k = x.shape
    n = w_all.shape[2]
    return pl.pallas_call(
        functools.partial(_mm_kernel, act=act),
        grid=(n // tn, m // tm),
        in_specs=[pl.BlockSpec((tm, k), lambda j, i: (i, 0)),
                  pl.BlockSpec((None, k, tn), lambda j, i: (widx, 0, j))],
        out_specs=pl.BlockSpec((tm, tn), lambda j, i: (i, j)),
        out_shape=jax.ShapeDtypeStruct((m, n), out_dtype),
        scratch_shapes=[pltpu.VMEM((k, tn), BF16)],
        compiler_params=_params("arbitrary", "arbitrary"),
        name="matmul",
    )(x, w_all)


def _mm_res_kernel(x_ref, w_ref, r_ref, g_ref, ng_ref, sh_ref, sc_ref, o_ref, h_ref):
    for c in range(x_ref.shape[0] // MM_ROWS):
        rows = slice(c * MM_ROWS, (c + 1) * MM_ROWS)
        xn = r_ref[rows, :] + g_ref[0] * _dot(x_ref[rows, :], w_ref[...])
        o_ref[rows, :] = xn
        h_ref[rows, :] = _norm_modulate(xn, ng_ref[0], sh_ref[0], sc_ref[0]).astype(h_ref.dtype)


def _matmul_residual(x, w, res, mod, g2, layer, n_prompt_rows, tm=512):
    m, k = x.shape
    n = w.shape[1]
    return pl.pallas_call(
        _mm_res_kernel,
        grid=(m // tm,),
        in_specs=[pl.BlockSpec((tm, k), lambda i: (i, 0)),
                  pl.BlockSpec((k, n), lambda i: (0, 0)),
                  pl.BlockSpec((tm, n), lambda i: (i, 0)),
                  _mod_spec(layer, 2, tm, n_prompt_rows, 0),
                  pl.BlockSpec((1, 1, n), lambda i: (layer, 0, 0)),
                  _mod_spec(layer, 3, tm, n_prompt_rows, 0),
                  _mod_spec(layer, 4, tm, n_prompt_rows, 0)],
        out_specs=[pl.BlockSpec((tm, n), lambda i: (i, 0)),
                   pl.BlockSpec((tm, n), lambda i: (i, 0))],
        out_shape=[jax.ShapeDtypeStruct((m, n), F32), jax.ShapeDtypeStruct((m, n), BF16)],
        input_output_aliases={2: 0},
        compiler_params=_params("arbitrary"),
        name="matmul_residual",
    )(x, w, res, mod, g2, mod, mod)


def _softmax_parts(parts, sink=None):
    m = parts[0].max(axis=-1, keepdims=True)
    for s in parts[1:]:
        m = jnp.maximum(m, s.max(axis=-1, keepdims=True))
    if sink is not None:
        m = jnp.maximum(m, sink)
    ps = [jnp.exp(s - m) for s in parts]
    den = ps[0].sum(axis=-1, keepdims=True)
    for p in ps[1:]:
        den = den + p.sum(axis=-1, keepdims=True)
    if sink is not None:
        den = den + jnp.exp(sink - m)
    return ps, 1.0 / den


def _attn_ctx_kernel(sink_ref, q_ref, k_ref, v_ref, o_ref, *, rep, use_sink):
    k_b = v_b = None
    for h in range(N_HEADS):
        g = h // rep
        if h % rep == 0:
            k_b = k_ref[:, g * HEAD_DIM:(g + 1) * HEAD_DIM].astype(BF16)
            v_b = v_ref[:, g * HEAD_DIM:(g + 1) * HEAD_DIM].astype(BF16)
        q = (q_ref[:, h * HEAD_DIM:(h + 1) * HEAD_DIM] * ATTN_SCALE).astype(BF16)
        s = _dot_nt(q, k_b)
        (p,), inv = _softmax_parts([s], sink_ref[h] if use_sink else None)
        o = _dot(p.astype(BF16), v_b) * inv
        o_ref[:, h * HEAD_DIM:(h + 1) * HEAD_DIM] = o.astype(o_ref.dtype)


def _attn_ctx(qkv, sink, n_rows, n_batch, n_kv):
    wq = N_HEADS * HEAD_DIM
    wk = n_kv * HEAD_DIM
    use_sink = sink is not None
    if sink is None:
        sink = jnp.zeros((N_HEADS,), F32)
    return pl.pallas_call(
        functools.partial(_attn_ctx_kernel, rep=N_HEADS // n_kv, use_sink=use_sink),
        grid=(n_batch,),
        in_specs=[pl.BlockSpec(memory_space=pltpu.SMEM),
                  pl.BlockSpec((SEQ, wq), lambda b: (b, 0)),
                  pl.BlockSpec((SEQ, wk), lambda b: (b, wq // wk)),
                  pl.BlockSpec((SEQ, wk), lambda b: (b, wq // wk + 1))],
        out_specs=pl.BlockSpec((SEQ, wq), lambda b: (b, 0)),
        out_shape=jax.ShapeDtypeStruct((n_rows, wq), BF16),
        compiler_params=_params("arbitrary"),
        name="attn_ctx",
    )(sink, qkv, qkv, qkv)


def _rope(x, cos_t, sin_t):
    lane = lax.broadcasted_iota(jnp.int32, x.shape, 1)
    partner = jnp.where((lane & ROPE_FREQS) == 0,
                        pltpu.roll(x, HEAD_DIM - ROPE_FREQS, 1), pltpu.roll(x, ROPE_FREQS, 1))
    return x * cos_t + partner * sin_t


def _attn_win_kernel(sink_ref, q_ref, k_ref, v_ref, kc_ref, vc_ref, cq_ref, sq_ref, ck_ref, sk_ref,
                     _, o_ref, kr_ref):
    i = pl.program_id(1)

    @pl.when(i == 0)
    def _():
        for g in range(N_KV_HEADS):
            cols = slice(g * HEAD_DIM, (g + 1) * HEAD_DIM)
            kr_ref[:, cols] = _rope(k_ref[:, cols], ck_ref[...], sk_ref[...]).astype(BF16)

    start = pl.multiple_of(jnp.clip(i * WINDOW - WINDOW, 0, DEC_SEQ - WIN_SPAN), WINDOW)
    qpos = i * WINDOW + lax.broadcasted_iota(jnp.int32, (WINDOW, WIN_SPAN), 0)
    kpos = start + lax.broadcasted_iota(jnp.int32, (WINDOW, WIN_SPAN), 1)
    band = jnp.where(jnp.abs(qpos - kpos) <= WINDOW, 0.0, NEG_INF)
    cq, sq = cq_ref[...], sq_ref[...]
    for g in range(N_KV_HEADS):
        cols = slice(g * HEAD_DIM, (g + 1) * HEAD_DIM)
        kw = kr_ref[pl.ds(start, WIN_SPAN), cols]
        vw = v_ref[pl.ds(start, WIN_SPAN), cols].astype(BF16)
        kc = kc_ref[0, :, cols].astype(BF16)
        vc = vc_ref[0, :, cols].astype(BF16)
        heads = range(g * KV_REP, (g + 1) * KV_REP)
        qs = jnp.concatenate(
            [_rope(q_ref[:, h * HEAD_DIM:(h + 1) * HEAD_DIM], cq, sq) * ATTN_SCALE for h in heads],
            axis=0).astype(BF16)
        s_w = _dot_nt(qs, kw)
        s_c = _dot_nt(qs, kc)
        pcs, pws, invs = [], [], []
        for t in range(KV_REP * WINDOW // SOFTMAX_ROWS):
            rows = slice(t * SOFTMAX_ROWS, (t + 1) * SOFTMAX_ROWS)
            q0 = (t * SOFTMAX_ROWS) % WINDOW
            sink = sink_ref[g * KV_REP + t * SOFTMAX_ROWS // WINDOW]
            (p_c, p_w), inv = _softmax_parts([s_c[rows], s_w[rows] + band[q0:q0 + SOFTMAX_ROWS]], sink)
            pcs.append(p_c.astype(BF16))
            pws.append(p_w.astype(BF16))
            invs.append(inv)
        o = (_dot(jnp.concatenate(pcs, axis=0), vc) + _dot(jnp.concatenate(pws, axis=0), vw)
             ) * jnp.concatenate(invs, axis=0)
        for r, h in enumerate(heads):
            o_ref[:, h * HEAD_DIM:(h + 1) * HEAD_DIM] = o[r * WINDOW:(r + 1) * WINDOW].astype(o_ref.dtype)


def _rope_tables(n):
    t = np.arange(n)
    row = jnp.asarray(t // GRID_W, F32)
    col = jnp.asarray(t % GRID_W, F32)
    inv = ROPE_BASE ** (-jnp.arange(ROPE_FREQS, dtype=F32) / ROPE_FREQS)
    ar, ac = row[:, None] * inv, col[:, None] * inv
    cos_t = jnp.concatenate([jnp.cos(ar), jnp.cos(ar), jnp.cos(ac), jnp.cos(ac)], axis=1)
    sin_t = jnp.concatenate([-jnp.sin(ar), jnp.sin(ar), -jnp.sin(ac), jnp.sin(ac)], axis=1)
    return cos_t, sin_t


def _attn_win(qkv, kc, vc, sink, o_buf, n_prompt_rows, n_batch):
    wq = N_HEADS * HEAD_DIM
    wk = N_KV_HEADS * HEAD_DIM
    cos_t, sin_t = _rope_tables(DEC_SEQ)
    nqb = DEC_SEQ // WINDOW
    qb0 = n_prompt_rows // WINDOW
    sb0 = n_prompt_rows // DEC_SEQ
    kc = kc.reshape(n_batch, -1, wk)
    vc = vc.reshape(n_batch, -1, wk)
    past = kc.shape[1]
    return pl.pallas_call(
        _attn_win_kernel,
        grid=(n_batch, nqb),
        in_specs=[pl.BlockSpec(memory_space=pltpu.SMEM),
                  pl.BlockSpec((WINDOW, wq), lambda b, i: (qb0 + b * nqb + i, 0)),
                  pl.BlockSpec((DEC_SEQ, wk), lambda b, i: (sb0 + b, wq // wk)),
                  pl.BlockSpec((DEC_SEQ, wk), lambda b, i: (sb0 + b, wq // wk + 1)),
                  pl.BlockSpec((1, past, wk), lambda b, i: (b, 0, 0)),
                  pl.BlockSpec((1, past, wk), lambda b, i: (b, 0, 0)),
                  pl.BlockSpec((WINDOW, HEAD_DIM), lambda b, i: (i, 0)),
                  pl.BlockSpec((WINDOW, HEAD_DIM), lambda b, i: (i, 0)),
                  pl.BlockSpec((DEC_SEQ, HEAD_DIM), lambda b, i: (0, 0)),
                  pl.BlockSpec((DEC_SEQ, HEAD_DIM), lambda b, i: (0, 0)),
                  pl.BlockSpec(memory_space=pl.ANY)],
        out_specs=pl.BlockSpec((WINDOW, wq), lambda b, i: (qb0 + b * nqb + i, 0)),
        out_shape=jax.ShapeDtypeStruct(o_buf.shape, o_buf.dtype),
        scratch_shapes=[pltpu.VMEM((DEC_SEQ, wk), BF16)],
        input_output_aliases={10: 0},
        compiler_params=_params("arbitrary", "arbitrary"),
        name="attn_win",
    )(sink, qkv, qkv, qkv, kc, vc, cos_t, sin_t, cos_t, sin_t, o_buf)


def _na_row_start(r):
    return min(max(r - NA_ROWS // 2, 0), GRID_ROWS - NA_ROWS)


def _attn_na_kernel(q_ref, k_ref, v_ref, kc_ref, vc_ref, bias_ref, _, o_ref):
    q = (q_ref[...] * ATTN_SCALE).astype(BF16)
    k = k_ref[...].astype(BF16)
    v = v_ref[...].astype(BF16)
    kc = kc_ref[0].astype(BF16)
    vc = vc_ref[0].astype(BF16)
    s_ctx = _dot_nt(q, kc)
    span = NA_ROWS * GRID_W
    rows = [slice(r * GRID_W, (r + 1) * GRID_W) for r in range(GRID_ROWS)]
    keys = [slice(_na_row_start(r) * GRID_W, _na_row_start(r) * GRID_W + span) for r in range(GRID_ROWS)]
    s_nbr = [_dot_nt(q[rows[r]], k[keys[r]]) + bias_ref[0, _na_row_start(r) - r + NA_ROWS - 1]
             for r in range(GRID_ROWS)]
    probs = []
    for r in range(GRID_ROWS):
        (p_c, p_n), inv = _softmax_parts([s_ctx[rows[r]], s_nbr[r]])
        probs.append((p_c.astype(BF16), p_n.astype(BF16), inv))
    for r, (p_c, p_n, inv) in enumerate(probs):
        o = (_dot(p_c, vc) + _dot(p_n, v[keys[r]])) * inv
        o_ref[rows[r], :] = o.astype(o_ref.dtype)


def _na_bias_table(rpb):
    cq = np.arange(GRID_W)[:, None]
    ck = np.arange(GRID_W)[None, :]
    cs = np.clip(cq - NA_COLS // 2, 0, GRID_W - NA_COLS)
    colmask = (ck >= cs) & (ck < cs + NA_COLS)
    dc = np.clip(ck - cq + NA_COLS - 1, 0, 2 * NA_COLS - 2)
    full = jnp.where(colmask, rpb[:, :, dc].astype(F32), NEG_INF)
    tabs = [jnp.transpose(full[:, off:off + NA_ROWS], (0, 2, 1, 3)).reshape(N_HEADS, GRID_W, NA_ROWS * GRID_W)
            for off in range(NA_ROWS)]
    return jnp.stack(tabs, axis=1)


def _attn_na(qkv, kc, vc, rpb, o_buf, n_prompt_rows, n_batch):
    sb0 = n_prompt_rows // DEC_SEQ
    kc = kc.reshape(n_batch, -1, N_HEADS * HEAD_DIM)
    vc = vc.reshape(n_batch, -1, N_HEADS * HEAD_DIM)
    past = kc.shape[1]
    bias = _na_bias_table(rpb)
    return pl.pallas_call(
        _attn_na_kernel,
        grid=(N_HEADS, n_batch),
        in_specs=[pl.BlockSpec((DEC_SEQ, HEAD_DIM), lambda h, b: (sb0 + b, h)),
                  pl.BlockSpec((DEC_SEQ, HEAD_DIM), lambda h, b: (sb0 + b, N_HEADS + h)),
                  pl.BlockSpec((DEC_SEQ, HEAD_DIM), lambda h, b: (sb0 + b, 2 * N_HEADS + h)),
                  pl.BlockSpec((1, past, HEAD_DIM), lambda h, b: (b, 0, h)),
                  pl.BlockSpec((1, past, HEAD_DIM), lambda h, b: (b, 0, h)),
                  pl.BlockSpec((1, NA_ROWS, GRID_W, NA_ROWS * GRID_W), lambda h, b: (h, 0, 0, 0)),
                  pl.BlockSpec(memory_space=pl.ANY)],
        out_specs=pl.BlockSpec((DEC_SEQ, HEAD_DIM), lambda h, b: (sb0 + b, h)),
        out_shape=jax.ShapeDtypeStruct(o_buf.shape, o_buf.dtype),
        input_output_aliases={6: 0},
        compiler_params=_params("arbitrary", "arbitrary"),
        name="attn_na",
    )(qkv, qkv, qkv, kc, vc, bias, o_buf)


def _sgu_kernel(z_ref, vg_ref, ws_ref, bs_ref, o_ref):
    sg = z_ref.shape[1] // 2
    v = z_ref[:, sg:].astype(F32)
    vn = v * lax.rsqrt(jnp.mean(v * v, axis=-1, keepdims=True) + NORM_EPS) * vg_ref[...]
    for g in range(SG_GROUPS):
        cols = slice(g * CHUNK, (g + 1) * CHUNK)
        f = _dot(ws_ref[g].astype(BF16), vn[:, cols].astype(BF16)) + bs_ref[:, g:g + 1]
        o_ref[:, cols] = (z_ref[:, cols].astype(F32) * f).astype(o_ref.dtype)


def _sgu(z, v_g, w_s, b_s):
    n, two_sg = z.shape
    sg = two_sg // 2
    return pl.pallas_call(
        _sgu_kernel,
        grid=(n // CHUNK,),
        in_specs=[pl.BlockSpec((CHUNK, two_sg), lambda i: (i, 0)),
                  pl.BlockSpec((1, sg), lambda i: (0, 0)),
                  pl.BlockSpec((SG_GROUPS, CHUNK, CHUNK), lambda i: (0, 0, 0)),
                  pl.BlockSpec((CHUNK, SG_GROUPS), lambda i: (0, 0))],
        out_specs=pl.BlockSpec((CHUNK, sg), lambda i: (i, 0)),
        out_shape=jax.ShapeDtypeStruct((n, sg), BF16),
        compiler_params=_params("arbitrary"),
        name="sgu",
    )(z, v_g.reshape(1, sg), w_s, b_s.T)


def _route_kernel(h_ref, wr_ref, *rest, seq, cap, aliased):
    if aliased:
        rest = rest[2:]
    xe_ref, gate_ref, pt_ref = rest
    h = h_ref[...]
    w = wr_ref[...]
    w_hi = w.astype(BF16)
    w_lo = (w - w_hi.astype(F32)).astype(BF16)
    logits = _dot(h, w_hi) + _dot(h, w_lo)
    lane = lax.broadcasted_iota(jnp.int32, (seq, LANES), 1)
    logits = jnp.where(lane < N_EXPERTS, logits, NEG_INF)
    ex = jnp.exp(logits - logits.max(axis=-1, keepdims=True))
    aff = ex / ex.sum(axis=-1, keepdims=True)

    def count(mask):
        ones = jnp.where(mask, 1.0, 0.0).reshape(COUNT_SPLIT, seq // COUNT_SPLIT, LANES)
        return jnp.sum(jnp.sum(ones, axis=0), axis=0, keepdims=True)

    def refine(i, ans):
        cand = ans | jnp.left_shift(jnp.int32(1), 30 - i)
        return jnp.where(count(aff >= pltpu.bitcast(cand, F32)) >= cap, cand, ans)

    thr = lax.fori_loop(0, 31, refine, jnp.zeros((1, LANES), jnp.int32))
    ge = aff >= pltpu.bitcast(thr, F32)
    gt = aff >= pltpu.bitcast(thr + 1, F32)
    gt_f = jnp.where(gt, 1.0, 0.0)
    eq_f = jnp.where(ge, 1.0, 0.0) - gt_f
    need = cap - count(gt)
    tri = jnp.where(lax.broadcasted_iota(jnp.int32, (seq, seq), 1)
                    < lax.broadcasted_iota(jnp.int32, (seq, seq), 0), 1.0, 0.0).astype(BF16)
    eq_before = _dot(tri, eq_f.astype(BF16))
    sel = gt | (ge & (eq_before < need))
    slot = _dot(tri, jnp.where(sel, 1.0, 0.0).astype(BF16))
    tgt = jnp.where(sel, slot, UNSELECTED)
    tgt_t = tgt.T
    aff_t = aff.T

    sub = lax.broadcasted_iota(jnp.int32, (cap, seq), 0).astype(F32)
    group = max(1, 256 // cap)
    for e0 in range(0, N_EXPERTS, group):
        hits = [sub == tgt_t[e:e + 1, :] for e in range(e0, e0 + group)]
        for k, hit in enumerate(hits):
            gate_ref[e0 + k] = jnp.sum(jnp.where(hit, aff_t[e0 + k:e0 + k + 1, :], 0.0), axis=1, keepdims=True)
        onehot = jnp.concatenate([jnp.where(hit, 1.0, 0.0) for hit in hits], axis=0).astype(BF16)
        rows = _dot(onehot, h).astype(BF16)
        for k in range(group):
            xe_ref[e0 + k] = rows[k * cap:(k + 1) * cap]

    lane_f = lane.astype(F32)
    for j in range(N_EXPERTS * cap // LANES):
        chunk = jnp.zeros((seq, LANES), F32)
        for e in range(j * LANES // cap, ((j + 1) * LANES - 1) // cap + 1):
            chunk = jnp.where(lane_f == tgt[:, e:e + 1] + float(e * cap - j * LANES), 1.0, chunk)
        pt_ref[0, :, j * LANES:(j + 1) * LANES] = chunk.astype(BF16)


def _route_gather(h, w_router_pad, seq, n_batch, row0, slot0, n_slots, prev=None):
    d = h.shape[1]
    cap = EC_FACTOR * seq // N_EXPERTS
    hb0 = row0 // seq
    sb0 = slot0 // cap
    in_specs = [pl.BlockSpec((seq, d), lambda b: (hb0 + b, 0)),
                pl.BlockSpec((d, LANES), lambda b: (0, 0))]
    args = [h, w_router_pad]
    aliases = {}
    if prev is not None:
        in_specs += [pl.BlockSpec(memory_space=pl.ANY), pl.BlockSpec(memory_space=pl.ANY)]
        args += list(prev)
        aliases = {2: 0, 3: 1}
    return pl.pallas_call(
        functools.partial(_route_kernel, seq=seq, cap=cap, aliased=prev is not None),
        grid=(n_batch,),
        in_specs=in_specs,
        out_specs=[pl.BlockSpec((N_EXPERTS, cap, d), lambda b: (0, sb0 + b, 0)),
                   pl.BlockSpec((N_EXPERTS, cap, 1), lambda b: (0, sb0 + b, 0)),
                   pl.BlockSpec((1, seq, N_EXPERTS * cap), lambda b: (b, 0, 0))],
        out_shape=[jax.ShapeDtypeStruct((N_EXPERTS, n_slots, d), BF16),
                   jax.ShapeDtypeStruct((N_EXPERTS, n_slots, 1), F32),
                   jax.ShapeDtypeStruct((n_batch, seq, N_EXPERTS * cap), BF16)],
        input_output_aliases=aliases,
        compiler_params=_params("arbitrary"),
        name="route_gather",
    )(*args)


FFN_ROWS = 256


def _ffn_up_kernel(x_ref, wg_ref, wu_ref, o_ref, wgb_ref, wub_ref):
    wgb_ref[...] = wg_ref[0].astype(BF16)
    wub_ref[...] = wu_ref[0].astype(BF16)
    for m in range(x_ref.shape[1] // FFN_ROWS):
        rows = slice(m * FFN_ROWS, (m + 1) * FFN_ROWS)
        x = x_ref[0, rows, :]
        o_ref[0, rows, :] = (_silu(_dot(x, wgb_ref[...])) * _dot(x, wub_ref[...])).astype(o_ref.dtype)


def _ffn_up(xe, w_gate, w_up, layer, tf=512):
    ne, r, d = xe.shape
    f = w_gate.shape[3]
    return pl.pallas_call(
        _ffn_up_kernel,
        grid=(ne, f // tf),
        in_specs=[pl.BlockSpec((1, r, d), lambda e, j: (e, 0, 0)),
                  pl.BlockSpec((None, 1, d, tf), lambda e, j: (layer, e, 0, j)),
                  pl.BlockSpec((None, 1, d, tf), lambda e, j: (layer, e, 0, j))],
        out_specs=pl.BlockSpec((1, r, tf), lambda e, j: (e, 0, j)),
        out_shape=jax.ShapeDtypeStruct((ne, r, f), BF16),
        scratch_shapes=[pltpu.VMEM((d, tf), BF16), pltpu.VMEM((d, tf), BF16)],
        compiler_params=_params("arbitrary", "arbitrary"),
        name="ffn_up",
    )(xe, w_gate, w_up)


def _ffn_down_kernel(h_ref, wd_ref, g_ref, o_ref, wdb_ref):
    wdb_ref[...] = wd_ref[0].astype(BF16)
    for m in range(h_ref.shape[1] // FFN_ROWS):
        rows = slice(m * FFN_ROWS, (m + 1) * FFN_ROWS)
        o_ref[0, rows, :] = (_dot(h_ref[0, rows, :], wdb_ref[...]) * g_ref[0, rows, :]).astype(o_ref.dtype)


def _ffn_down(hmid, w_down, gate, layer, tn=1024):
    ne, r, f = hmid.shape
    d = w_down.shape[3]
    return pl.pallas_call(
        _ffn_down_kernel,
        grid=(ne, d // tn),
        in_specs=[pl.BlockSpec((1, r, f), lambda e, j: (e, 0, 0)),
                  pl.BlockSpec((None, 1, f, tn), lambda e, j: (layer, e, 0, j)),
                  pl.BlockSpec((1, r, 1), lambda e, j: (e, 0, 0))],
        out_specs=pl.BlockSpec((1, r, tn), lambda e, j: (e, 0, j)),
        out_shape=jax.ShapeDtypeStruct((ne, r, d), BF16),
        scratch_shapes=[pltpu.VMEM((f, tn), BF16)],
        compiler_params=_params("arbitrary", "arbitrary"),
        name="ffn_down",
    )(hmid, w_down, gate)


COMBINE_ROWS = 512


def _moe_tokens(pt_ref, y_ref, x_ref, g_ref):
    ne, cap, d = y_ref.shape
    return x_ref[...] + g_ref[0] * _dot(pt_ref[0], y_ref[...].reshape(ne * cap, d))


def _combine_next_kernel(pt_ref, y_ref, x_ref, g_ref, ng_ref, sh_ref, sc_ref, *rest):
    o_ref, h_ref = rest[-2:]
    xn = _moe_tokens(pt_ref, y_ref, x_ref, g_ref)
    o_ref[...] = xn
    h_ref[...] = _norm_modulate(xn, ng_ref[0], sh_ref[0], sc_ref[0]).astype(h_ref.dtype)


def _combine_final_kernel(pt_ref, y_ref, x_ref, g_ref, ng_ref, o_ref):
    xn = _moe_tokens(pt_ref, y_ref, x_ref, g_ref)
    o_ref[...] = xn * lax.rsqrt(jnp.mean(xn * xn, axis=-1, keepdims=True) + NORM_EPS) * ng_ref[...]


def _combine(pt, y, x, mod, layer, seq, row0, slot0, n_prompt_rows, nxt, h_prev=None):
    n_batch = pt.shape[0]
    d = x.shape[1]
    cap = EC_FACTOR * seq // N_EXPERTS
    tr = min(seq, COMBINE_ROWS)
    per_seq = seq // tr
    xt0 = row0 // tr
    sb0 = slot0 // cap

    def tile(b, t):
        return xt0 + b * per_seq + t

    def mod_spec(lyr, which):
        return pl.BlockSpec(
            (1, 1, d), lambda b, t: ((lyr * COND_ROWS + _cond_of_tile(tile(b, t), tr, n_prompt_rows)) * 6 + which,
                                     0, 0))

    in_specs = [pl.BlockSpec((1, tr, N_EXPERTS * cap), lambda b, t: (b, t, 0)),
                pl.BlockSpec((N_EXPERTS, cap, d), lambda b, t: (0, sb0 + b, 0)),
                pl.BlockSpec((tr, d), lambda b, t: (tile(b, t), 0)),
                mod_spec(layer, 5)]
    if isinstance(nxt, tuple):
        g_next, l_next = nxt
        in_specs += [pl.BlockSpec((1, 1, d), lambda b, t: (l_next, 0, 0)), mod_spec(l_next, 0), mod_spec(l_next, 1)]
        args = [pt, y, x, mod, g_next, mod, mod]
        aliases = {2: 0}
        if h_prev is not None:
            in_specs.append(pl.BlockSpec(memory_space=pl.ANY))
            args.append(h_prev)
            aliases[7] = 1
        return pl.pallas_call(
            _combine_next_kernel,
            grid=(n_batch, per_seq),
            in_specs=in_specs,
            out_specs=[pl.BlockSpec((tr, d), lambda b, t: (tile(b, t), 0)),
                       pl.BlockSpec((tr, d), lambda b, t: (tile(b, t), 0))],
            out_shape=[jax.ShapeDtypeStruct(x.shape, x.dtype), jax.ShapeDtypeStruct(x.shape, BF16)],
            input_output_aliases=aliases,
            compiler_params=_params("arbitrary", "arbitrary"),
            name="moe_combine",
        )(*args)
    in_specs.append(pl.BlockSpec((1, d), lambda b, t: (0, 0)))
    return pl.pallas_call(
        _combine_final_kernel,
        grid=(n_batch, per_seq),
        in_specs=in_specs,
        out_specs=pl.BlockSpec((tr, d), lambda b, t: (b * per_seq + t, 0)),
        out_shape=jax.ShapeDtypeStruct((n_batch * seq, d), F32),
        compiler_params=_params("arbitrary", "arbitrary"),
        name="moe_combine_final",
    )(pt, y, x, mod, nxt)


def _moe(h, x, mod, layer, w_router, w_gate, w_up, w_down, n_prompt, n_latent, nxt):
    n_prompt_rows = n_prompt * SEQ
    cap_p = EC_FACTOR * SEQ // N_EXPERTS
    cap_s = EC_FACTOR * DEC_SEQ // N_EXPERTS
    slots_p = n_prompt * cap_p
    n_slots = slots_p + n_latent * cap_s
    wr = jnp.pad(w_router[layer], ((0, 0), (0, LANES - N_EXPERTS)))
    xe, gate, pt_p = _route_gather(h, wr, SEQ, n_prompt, 0, 0, n_slots)
    xe, gate, pt_s = _route_gather(h, wr, DEC_SEQ, n_latent, n_prompt_rows, slots_p, n_slots, prev=(xe, gate))
    y = _ffn_down(_ffn_up(xe, w_gate, w_up, layer), w_down, gate, layer)
    if isinstance(nxt, tuple):
        x, h = _combine(pt_p, y, x, mod, layer, SEQ, 0, 0, n_prompt_rows, nxt)
        return _combine(pt_s, y, x, mod, layer, DEC_SEQ, n_prompt_rows, slots_p, n_prompt_rows, nxt, h_prev=h)
    return (_combine(pt_p, y, x, mod, layer, SEQ, 0, 0, n_prompt_rows, nxt),
            _combine(pt_s, y, x, mod, layer, DEC_SEQ, n_prompt_rows, slots_p, n_prompt_rows, nxt))


def kernel(x_prompt, x_sample, cache_k_0, cache_v_0, cache_k_1, cache_v_1, cache_k_3, cache_v_3, c, c_ctx, norm1_g, norm2_g, w_mod, b_mod, a_w_qkv, a_sink, a_w_o, b_w_qkv, b_rpb, b_w_o, c_w_in, c_v_g, c_w_s, c_b_s, c_w_out, w_router, w_gate, w_up, w_down, final_g):
    n_prompt, _, d = x_prompt.shape
    n_latent = x_sample.shape[0]
    depth = w_mod.shape[0]
    p_rows = n_prompt * SEQ
    s_rows = n_latent * DEC_SEQ
    caches = {0: (cache_k_0, cache_v_0), 1: (cache_k_1, cache_v_1), 3: (cache_k_3, cache_v_3)}

    cond = jnp.concatenate([c_ctx[None], c, jnp.zeros((COND_ROWS - 1 - n_latent, d), F32)], axis=0)
    mod = _adaln_all(cond, w_mod, b_mod).reshape(depth * COND_ROWS * 6, 1, d)
    g1 = norm1_g.reshape(depth, 1, d)
    g2 = norm2_g.reshape(depth, 1, d)
    x = jnp.concatenate([x_prompt.reshape(p_rows, d), x_sample.reshape(s_rows, d)], axis=0)

    new_state = []
    h = _modulate(x, g1, mod, 0, 0, 1, p_rows)
    for l in range(depth):
        kind, j = l % 3, l // 3
        if kind == 0:
            qkv = _matmul(h, a_w_qkv, j, F32)
            o = _attn_ctx(qkv, a_sink[j], p_rows + s_rows, n_prompt, N_KV_HEADS)
            o = _attn_win(qkv, *caches[l], a_sink[j], o, p_rows, n_latent)
            w_out = a_w_o[j]
            wq, wk = N_HEADS * HEAD_DIM, N_KV_HEADS * HEAD_DIM
            new_state += [qkv[:p_rows, wq:wq + wk].reshape(n_prompt, SEQ, N_KV_HEADS, HEAD_DIM),
                          qkv[:p_rows, wq + wk:].reshape(n_prompt, SEQ, N_KV_HEADS, HEAD_DIM)]
        elif kind == 1:
            qkv = _matmul(h, b_w_qkv, j, F32)
            o = _attn_ctx(qkv, None, p_rows + s_rows, n_prompt, N_HEADS)
            o = _attn_na(qkv, *caches[l], b_rpb[j], o, p_rows, n_latent)
            w_out = b_w_o[j]
            new_state += [qkv[:p_rows, d:2 * d].reshape(n_prompt, SEQ, N_HEADS, HEAD_DIM),
                          qkv[:p_rows, 2 * d:].reshape(n_prompt, SEQ, N_HEADS, HEAD_DIM)]
        else:
            z = _matmul(h, c_w_in, j, BF16, act="gelu")
            o = _sgu(z, c_v_g[j], c_w_s[j], c_b_s[j])
            w_out = c_w_out[j]
        x, h = _matmul_residual(o, w_out.astype(BF16), x, mod, g2, l, p_rows)
        if l + 1 < depth:
            x, h = _moe(h, x, mod, l, w_router, w_gate, w_up, w_down, n_prompt, n_latent, (g1, l + 1))
    y_prompt, y_sample = _moe(h, x, mod, depth - 1, w_router, w_gate, w_up, w_down, n_prompt, n_latent,
                              final_g.reshape(1, d))
    return (y_prompt.reshape(n_prompt, SEQ, d), y_sample.reshape(n_latent, DEC_SEQ, d), *new_state)
```

```python
import functools

import numpy as np
import jax
import jax.numpy as jnp
from jax import lax
from jax.experimental import pallas as pl
from jax.experimental.pallas import tpu as pltpu

F32 = jnp.float32
BF16 = jnp.bfloat16

D_MODEL = 2048
HEAD_DIM = 128
N_HEADS = D_MODEL // HEAD_DIM
N_KV_HEADS = 4
KV_REP = N_HEADS // N_KV_HEADS
ATTN_SCALE = HEAD_DIM ** -0.5
SEQ = 256
DEC_SEQ = 1024
WINDOW = 128
WIN_SPAN = 3 * WINDOW
GRID_W = 64
GRID_ROWS = DEC_SEQ // GRID_W
NA_ROWS = 8
NA_COLS = 16
ROPE_BASE = 10000.0
ROPE_FREQS = HEAD_DIM // 4
CHUNK = 128
SG_GROUPS = 16
N_EXPERTS = 16
EC_FACTOR = 2
NORM_EPS = 1e-6
NEG_INF = -1e30
COND_ROWS = 16
LANES = 128
SOFTMAX_ROWS = 64
MM_ROWS = 256
COUNT_SPLIT = 8
UNSELECTED = -4096.0
VMEM_LIMIT = 56 * 1024 * 1024


def _params(*sem):
    return pltpu.CompilerParams(dimension_semantics=sem, vmem_limit_bytes=VMEM_LIMIT)


def _dot(a, b):
    return jnp.dot(a, b, preferred_element_type=F32)


def _dot_nt(a, b):
    return lax.dot_general(a, b, (((1,), (1,)), ((), ())), preferred_element_type=F32)


def _silu(x):
    return x / (1.0 + jnp.exp(-x))


def _cond_of_tile(i, tm, n_prompt_rows):
    n_p = n_prompt_rows // tm
    per_b = DEC_SEQ // tm
    return jnp.where(i < n_p, 0, 1 + (i - n_p) // per_b)


def _adaln_kernel(c_ref, w_ref, b_ref, o_ref):
    s = _silu(c_ref[...]).astype(BF16)
    o_ref[0] = _dot(s, w_ref[0].astype(BF16)) + b_ref[0]


def _adaln_all(cond, w_mod, b_mod):
    depth, d, n = w_mod.shape
    tn = 1024
    return pl.pallas_call(
        _adaln_kernel,
        grid=(depth, n // tn),
        in_specs=[pl.BlockSpec((COND_ROWS, d), lambda l, j: (0, 0)),
                  pl.BlockSpec((1, d, tn), lambda l, j: (l, 0, j)),
                  pl.BlockSpec((1, 1, tn), lambda l, j: (l, 0, j))],
        out_specs=pl.BlockSpec((1, COND_ROWS, tn), lambda l, j: (l, 0, j)),
        out_shape=jax.ShapeDtypeStruct((depth, COND_ROWS, n), F32),
        compiler_params=_params("arbitrary", "arbitrary"),
        name="adaln",
    )(cond, w_mod, b_mod.reshape(depth, 1, n))


def _mod_spec(layer, which, tm, n_prompt_rows, row_axis, tn=D_MODEL, col_axis=None):
    def index(*ids):
        cond = _cond_of_tile(ids[row_axis], tm, n_prompt_rows)
        col = 0 if col_axis is None else ids[col_axis]
        return ((layer * COND_ROWS + cond) * 6 + which, 0, col)
    return pl.BlockSpec((1, 1, tn), index)


def _norm_modulate(x, ng, shift, scale):
    y = x * lax.rsqrt(jnp.mean(x * x, axis=-1, keepdims=True) + NORM_EPS) * ng
    return y * (1.0 + scale) + shift


def _modulate_kernel(x_ref, g_ref, sh_ref, sc_ref, *rest):
    xo_ref, h_ref = rest[-2:]
    x = x_ref[...]
    xo_ref[...] = x
    h_ref[...] = _norm_modulate(x, g_ref[0], sh_ref[0], sc_ref[0]).astype(h_ref.dtype)


def _enter_stream(x_part, row0, n_rows, g_all, mod, n_prompt_rows, prev=None):
    n, d = x_part.shape
    tm = 512
    t0 = row0 // tm
    in_specs = [pl.BlockSpec((tm, d), lambda i: (i, 0)),
                pl.BlockSpec((1, 1, d), lambda i: (0, 0, 0)),
                pl.BlockSpec((1, 1, d), lambda i: ((_cond_of_tile(t0 + i, tm, n_prompt_rows)) * 6, 0, 0)),
                pl.BlockSpec((1, 1, d), lambda i: ((_cond_of_tile(t0 + i, tm, n_prompt_rows)) * 6 + 1, 0, 0))]
    args = [x_part, g_all, mod, mod]
    aliases = {}
    if prev is not None:
        in_specs += [pl.BlockSpec(memory_space=pl.ANY), pl.BlockSpec(memory_space=pl.ANY)]
        args += list(prev)
        aliases = {4: 0, 5: 1}
    return pl.pallas_call(
        _modulate_kernel,
        grid=(n // tm,),
        in_specs=in_specs,
        out_specs=[pl.BlockSpec((tm, d), lambda i: (t0 + i, 0)),
                   pl.BlockSpec((tm, d), lambda i: (t0 + i, 0))],
        out_shape=[jax.ShapeDtypeStruct((n_rows, d), F32), jax.ShapeDtypeStruct((n_rows, d), BF16)],
        input_output_aliases=aliases,
        compiler_params=_params("arbitrary"),
        name="enter_stream",
    )(*args)


def _mm_kernel(x_ref, w_ref, o_ref, wb_ref, *, act):
    @pl.when(pl.program_id(1) == 0)
    def _():
        wb_ref[...] = w_ref[...].astype(BF16)

    for c in range(x_ref.shape[0] // MM_ROWS):
        rows = slice(c * MM_ROWS, (c + 1) * MM_ROWS)
        acc = _dot(x_ref[rows, :], wb_ref[...])
        if act == "gelu":
            acc = jax.nn.gelu(acc)
        o_ref[rows, :] = acc.astype(o_ref.dtype)


def _matmul(x, w_all, widx, out_dtype, act=None, tm=1024, tn=1024):
    m, k = x.shape
    n = w_all.shape[2]
    return pl.pallas_call(
        functools.partial(_mm_kernel, act=act),
        grid=(n // tn, m // tm),
        in_specs=[pl.BlockSpec((tm, k), lambda j, i: (i, 0)),
                  pl.BlockSpec((None, k, tn), lambda j, i: (widx, 0, j))],
        out_specs=pl.BlockSpec((tm, tn), lambda j, i: (i, j)),
        out_shape=jax.ShapeDtypeStruct((m, n), out_dtype),
        scratch_shapes=[pltpu.VMEM((k, tn), BF16)],
        compiler_params=_params("arbitrary", "arbitrary"),
        name="matmul",
    )(x, w_all)


def _mm_res_kernel(x_ref, w_ref, r_ref, g_ref, ng_ref, sh_ref, sc_ref, o_ref, h_ref):
    for c in range(x_ref.shape[0] // MM_ROWS):
        rows = slice(c * MM_ROWS, (c + 1) * MM_ROWS)
        xn = r_ref[rows, :] + g_ref[0] * _dot(x_ref[rows, :], w_ref[...])
        o_ref[rows, :] = xn
        h_ref[rows, :] = _norm_modulate(xn, ng_ref[0], sh_ref[0], sc_ref[0]).astype(h_ref.dtype)


def _matmul_residual(x, w, res, mod, g2, layer, n_prompt_rows, tm=512):
    m, k = x.shape
    n = w.shape[1]
    return pl.pallas_call(
        _mm_res_kernel,
        grid=(m // tm,),
        in_specs=[pl.BlockSpec((tm, k), lambda i: (i, 0)),
                  pl.BlockSpec((k, n), lambda i: (0, 0)),
                  pl.BlockSpec((tm, n), lambda i: (i, 0)),
                  _mod_spec(layer, 2, tm, n_prompt_rows, 0),
                  pl.BlockSpec((1, 1, n), lambda i: (layer, 0, 0)),
                  _mod_spec(layer, 3, tm, n_prompt_rows, 0),
                  _mod_spec(layer, 4, tm, n_prompt_rows, 0)],
        out_specs=[pl.BlockSpec((tm, n), lambda i: (i, 0)),
                   pl.BlockSpec((tm, n), lambda i: (i, 0))],
        out_shape=[jax.ShapeDtypeStruct((m, n), F32), jax.ShapeDtypeStruct((m, n), BF16)],
        input_output_aliases={2: 0},
        compiler_params=_params("arbitrary"),
        name="matmul_residual",
    )(x, w, res, mod, g2, mod, mod)


def _softmax_parts(parts, sink=None):
    m = parts[0].max(axis=-1, keepdims=True)
    for s in parts[1:]:
        m = jnp.maximum(m, s.max(axis=-1, keepdims=True))
    if sink is not None:
        m = jnp.maximum(m, sink)
    ps = [jnp.exp(s - m) for s in parts]
    den = ps[0].sum(axis=-1, keepdims=True)
    for p in ps[1:]:
        den = den + p.sum(axis=-1, keepdims=True)
    if sink is not None:
        den = den + jnp.exp(sink - m)
    return ps, 1.0 / den


def _attn_ctx_kernel(sink_ref, q_ref, k_ref, v_ref, o_ref, ks_ref, vs_ref, *, rep, use_sink):
    k_b = v_b = None
    for h in range(N_HEADS):
        g = h // rep
        if h % rep == 0:
            k_f = k_ref[:, g * HEAD_DIM:(g + 1) * HEAD_DIM]
            v_f = v_ref[:, g * HEAD_DIM:(g + 1) * HEAD_DIM]
            ks_ref[0, :, g, :] = k_f
            vs_ref[0, :, g, :] = v_f
            k_b = k_f.astype(BF16)
            v_b = v_f.astype(BF16)
        q = (q_ref[:, h * HEAD_DIM:(h + 1) * HEAD_DIM] * ATTN_SCALE).astype(BF16)
        s = _dot_nt(q, k_b)
        (p,), inv = _softmax_parts([s], sink_ref[h] if use_sink else None)
        o = _dot(p.astype(BF16), v_b) * inv
        o_ref[:, h * HEAD_DIM:(h + 1) * HEAD_DIM] = o.astype(o_ref.dtype)


def _attn_ctx(qkv, sink, n_rows, n_batch, n_kv):
    wq = N_HEADS * HEAD_DIM
    wk = n_kv * HEAD_DIM
    use_sink = sink is not None
    if sink is None:
        sink = jnp.zeros((N_HEADS,), F32)
    return pl.pallas_call(
        functools.partial(_attn_ctx_kernel, rep=N_HEADS // n_kv, use_sink=use_sink),
        grid=(n_batch,),
        in_specs=[pl.BlockSpec(memory_space=pltpu.SMEM),
                  pl.BlockSpec((SEQ, wq), lambda b: (b, 0)),
                  pl.BlockSpec((SEQ, wk), lambda b: (b, wq // wk)),
                  pl.BlockSpec((SEQ, wk), lambda b: (b, wq // wk + 1))],
        out_specs=[pl.BlockSpec((SEQ, wq), lambda b: (b, 0)),
                   pl.BlockSpec((1, SEQ, n_kv, HEAD_DIM), lambda b: (b, 0, 0, 0)),
                   pl.BlockSpec((1, SEQ, n_kv, HEAD_DIM), lambda b: (b, 0, 0, 0))],
        out_shape=[jax.ShapeDtypeStruct((n_rows, wq), BF16),
                   jax.ShapeDtypeStruct((n_batch, SEQ, n_kv, HEAD_DIM), F32),
                   jax.ShapeDtypeStruct((n_batch, SEQ, n_kv, HEAD_DIM), F32)],
        compiler_params=_params("arbitrary"),
        name="attn_ctx",
    )(sink, qkv, qkv, qkv)


def _rope(x, cos_t, sin_t):
    lane = lax.broadcasted_iota(jnp.int32, x.shape, 1)
    partner = jnp.where((lane & ROPE_FREQS) == 0,
                        pltpu.roll(x, HEAD_DIM - ROPE_FREQS, 1), pltpu.roll(x, ROPE_FREQS, 1))
    return x * cos_t + partner * sin_t


def _attn_win_kernel(sink_ref, q_ref, k_ref, v_ref, kc_ref, vc_ref, cq_ref, sq_ref, ck_ref, sk_ref,
                     _, o_ref, kr_ref, kcb_ref, vcb_ref):
    i = pl.program_id(1)

    @pl.when(i == 0)
    def _():
        for g in range(N_KV_HEADS):
            cols = slice(g * HEAD_DIM, (g + 1) * HEAD_DIM)
            kr_ref[:, cols] = _rope(k_ref[:, cols], ck_ref[...], sk_ref[...]).astype(BF16)
            kcb_ref[:, cols] = kc_ref[0, :, g, :].astype(BF16)
            vcb_ref[:, cols] = vc_ref[0, :, g, :].astype(BF16)

    start = pl.multiple_of(jnp.clip(i * WINDOW - WINDOW, 0, DEC_SEQ - WIN_SPAN), WINDOW)
    qpos = i * WINDOW + lax.broadcasted_iota(jnp.int32, (WINDOW, WIN_SPAN), 0)
    kpos = start + lax.broadcasted_iota(jnp.int32, (WINDOW, WIN_SPAN), 1)
    band = jnp.where(jnp.abs(qpos - kpos) <= WINDOW, 0.0, NEG_INF)
    cq, sq = cq_ref[...], sq_ref[...]
    for g in range(N_KV_HEADS):
        cols = slice(g * HEAD_DIM, (g + 1) * HEAD_DIM)
        kw = kr_ref[pl.ds(start, WIN_SPAN), cols]
        vw = v_ref[pl.ds(start, WIN_SPAN), cols].astype(BF16)
        kc = kcb_ref[:, cols]
        vc = vcb_ref[:, cols]
        heads = range(g * KV_REP, (g + 1) * KV_REP)
        qs = jnp.concatenate(
            [_rope(q_ref[:, h * HEAD_DIM:(h + 1) * HEAD_DIM], cq, sq) * ATTN_SCALE for h in heads],
            axis=0).astype(BF16)
        s_w = _dot_nt(qs, kw)
        s_c = _dot_nt(qs, kc)
        pcs, pws, invs = [], [], []
        for t in range(KV_REP * WINDOW // SOFTMAX_ROWS):
            rows = slice(t * SOFTMAX_ROWS, (t + 1) * SOFTMAX_ROWS)
            q0 = (t * SOFTMAX_ROWS) % WINDOW
            sink = sink_ref[g * KV_REP + t * SOFTMAX_ROWS // WINDOW]
            (p_c, p_w), inv = _softmax_parts([s_c[rows], s_w[rows] + band[q0:q0 + SOFTMAX_ROWS]], sink)
            pcs.append(p_c.astype(BF16))
            pws.append(p_w.astype(BF16))
            invs.append(inv)
        o = (_dot(jnp.concatenate(pcs, axis=0), vc) + _dot(jnp.concatenate(pws, axis=0), vw)
             ) * jnp.concatenate(invs, axis=0)
        for r, h in enumerate(heads):
            o_ref[:, h * HEAD_DIM:(h + 1) * HEAD_DIM] = o[r * WINDOW:(r + 1) * WINDOW].astype(o_ref.dtype)


def _rope_tables(n):
    t = np.arange(n)
    row = jnp.asarray(t // GRID_W, F32)
    col = jnp.asarray(t % GRID_W, F32)
    inv = ROPE_BASE ** (-jnp.arange(ROPE_FREQS, dtype=F32) / ROPE_FREQS)
    ar, ac = row[:, None] * inv, col[:, None] * inv
    cos_t = jnp.concatenate([jnp.cos(ar), jnp.cos(ar), jnp.cos(ac), jnp.cos(ac)], axis=1)
    sin_t = jnp.concatenate([-jnp.sin(ar), jnp.sin(ar), -jnp.sin(ac), jnp.sin(ac)], axis=1)
    return cos_t, sin_t


def _attn_win(qkv, kc, vc, sink, o_buf, n_prompt_rows, n_batch):
    wq = N_HEADS * HEAD_DIM
    wk = N_KV_HEADS * HEAD_DIM
    cos_t, sin_t = _rope_tables(DEC_SEQ)
    nqb = DEC_SEQ // WINDOW
    qb0 = n_prompt_rows // WINDOW
    sb0 = n_prompt_rows // DEC_SEQ
    past = kc.shape[1]
    return pl.pallas_call(
        _attn_win_kernel,
        grid=(n_batch, nqb),
        in_specs=[pl.BlockSpec(memory_space=pltpu.SMEM),
                  pl.BlockSpec((WINDOW, wq), lambda b, i: (qb0 + b * nqb + i, 0)),
                  pl.BlockSpec((DEC_SEQ, wk), lambda b, i: (sb0 + b, wq // wk)),
                  pl.BlockSpec((DEC_SEQ, wk), lambda b, i: (sb0 + b, wq // wk + 1)),
                  pl.BlockSpec((1, past, N_KV_HEADS, HEAD_DIM), lambda b, i: (b, 0, 0, 0)),
                  pl.BlockSpec((1, past, N_KV_HEADS, HEAD_DIM), lambda b, i: (b, 0, 0, 0)),
                  pl.BlockSpec((WINDOW, HEAD_DIM), lambda b, i: (i, 0)),
                  pl.BlockSpec((WINDOW, HEAD_DIM), lambda b, i: (i, 0)),
                  pl.BlockSpec((DEC_SEQ, HEAD_DIM), lambda b, i: (0, 0)),
                  pl.BlockSpec((DEC_SEQ, HEAD_DIM), lambda b, i: (0, 0)),
                  pl.BlockSpec(memory_space=pl.ANY)],
        out_specs=pl.BlockSpec((WINDOW, wq), lambda b, i: (qb0 + b * nqb + i, 0)),
        out_shape=jax.ShapeDtypeStruct(o_buf.shape, o_buf.dtype),
        scratch_shapes=[pltpu.VMEM((DEC_SEQ, wk), BF16), pltpu.VMEM((past, wk), BF16),
                        pltpu.VMEM((past, wk), BF16)],
        input_output_aliases={10: 0},
        compiler_params=_params("arbitrary", "arbitrary"),
        name="attn_win",
    )(sink, qkv, qkv, qkv, kc, vc, cos_t, sin_t, cos_t, sin_t, o_buf)


def _na_row_start(r):
    return min(max(r - NA_ROWS // 2, 0), GRID_ROWS - NA_ROWS)


def _attn_na_kernel(q_ref, k_ref, v_ref, kc_ref, vc_ref, bias_ref, _, o_ref):
    q = (q_ref[...] * ATTN_SCALE).astype(BF16)
    k = k_ref[...].astype(BF16)
    v = v_ref[...].astype(BF16)
    kc = kc_ref[0].astype(BF16)
    vc = vc_ref[0].astype(BF16)
    s_ctx = _dot_nt(q, kc)
    span = NA_ROWS * GRID_W
    rows = [slice(r * GRID_W, (r + 1) * GRID_W) for r in range(GRID_ROWS)]
    keys = [slice(_na_row_start(r) * GRID_W, _na_row_start(r) * GRID_W + span) for r in range(GRID_ROWS)]
    s_nbr = [_dot_nt(q[rows[r]], k[keys[r]]) + bias_ref[0, _na_row_start(r) - r + NA_ROWS - 1]
             for r in range(GRID_ROWS)]
    probs = []
    for r in range(GRID_ROWS):
        (p_c, p_n), inv = _softmax_parts([s_ctx[rows[r]], s_nbr[r]])
        probs.append((p_c.astype(BF16), p_n.astype(BF16), inv))
    for r, (p_c, p_n, inv) in enumerate(probs):
        o = (_dot(p_c, vc) + _dot(p_n, v[keys[r]])) * inv
        o_ref[rows[r], :] = o.astype(o_ref.dtype)


def _na_bias_table(rpb):
    cq = np.arange(GRID_W)[:, None]
    ck = np.arange(GRID_W)[None, :]
    cs = np.clip(cq - NA_COLS // 2, 0, GRID_W - NA_COLS)
    colmask = (ck >= cs) & (ck < cs + NA_COLS)
    dc = np.clip(ck - cq + NA_COLS - 1, 0, 2 * NA_COLS - 2)
    full = jnp.where(colmask, rpb[:, :, dc].astype(F32), NEG_INF)
    tabs = [jnp.transpose(full[:, off:off + NA_ROWS], (0, 2, 1, 3)).reshape(N_HEADS, GRID_W, NA_ROWS * GRID_W)
            for off in range(NA_ROWS)]
    return jnp.stack(tabs, axis=1)


def _attn_na(qkv, kc, vc, rpb, o_buf, n_prompt_rows, n_batch):
    sb0 = n_prompt_rows // DEC_SEQ
    kc = kc.reshape(n_batch, -1, N_HEADS * HEAD_DIM)
    vc = vc.reshape(n_batch, -1, N_HEADS * HEAD_DIM)
    past = kc.shape[1]
    bias = _na_bias_table(rpb)
    return pl.pallas_call(
        _attn_na_kernel,
        grid=(N_HEADS, n_batch),
        in_specs=[pl.BlockSpec((DEC_SEQ, HEAD_DIM), lambda h, b: (sb0 + b, h)),
                  pl.BlockSpec((DEC_SEQ, HEAD_DIM), lambda h, b: (sb0 + b, N_HEADS + h)),
                  pl.BlockSpec((DEC_SEQ, HEAD_DIM), lambda h, b: (sb0 + b, 2 * N_HEADS + h)),
                  pl.BlockSpec((1, past, HEAD_DIM), lambda h, b: (b, 0, h)),
                  pl.BlockSpec((1, past, HEAD_DIM), lambda h, b: (b, 0, h)),
                  pl.BlockSpec((1, NA_ROWS, GRID_W, NA_ROWS * GRID_W), lambda h, b: (h, 0, 0, 0)),
                  pl.BlockSpec(memory_space=pl.ANY)],
        out_specs=pl.BlockSpec((DEC_SEQ, HEAD_DIM), lambda h, b: (sb0 + b, h)),
        out_shape=jax.ShapeDtypeStruct(o_buf.shape, o_buf.dtype),
        input_output_aliases={6: 0},
        compiler_params=_params("arbitrary", "arbitrary"),
        name="attn_na",
    )(qkv, qkv, qkv, kc, vc, bias, o_buf)


def _sgu_kernel(z_ref, vg_ref, ws_ref, bs_ref, o_ref):
    sg = z_ref.shape[1] // 2
    v = z_ref[:, sg:].astype(F32)
    vn = v * lax.rsqrt(jnp.mean(v * v, axis=-1, keepdims=True) + NORM_EPS) * vg_ref[...]
    for g in range(SG_GROUPS):
        cols = slice(g * CHUNK, (g + 1) * CHUNK)
        f = _dot(ws_ref[g].astype(BF16), vn[:, cols].astype(BF16)) + bs_ref[:, g:g + 1]
        o_ref[:, cols] = (z_ref[:, cols].astype(F32) * f).astype(o_ref.dtype)


def _sgu(z, v_g, w_s, b_s):
    n, two_sg = z.shape
    sg = two_sg // 2
    return pl.pallas_call(
        _sgu_kernel,
        grid=(n // CHUNK,),
        in_specs=[pl.BlockSpec((CHUNK, two_sg), lambda i: (i, 0)),
                  pl.BlockSpec((1, sg), lambda i: (0, 0)),
                  pl.BlockSpec((SG_GROUPS, CHUNK, CHUNK), lambda i: (0, 0, 0)),
                  pl.BlockSpec((CHUNK, SG_GROUPS), lambda i: (0, 0))],
        out_specs=pl.BlockSpec((CHUNK, sg), lambda i: (i, 0)),
        out_shape=jax.ShapeDtypeStruct((n, sg), BF16),
        compiler_params=_params("arbitrary"),
        name="sgu",
    )(z, v_g.reshape(1, sg), w_s, b_s.T)


def _route_kernel(h_ref, wr_ref, *rest, seq, cap, aliased):
    if aliased:
        rest = rest[2:]
    xe_ref, gate_ref, pt_ref = rest
    h = h_ref[...]
    w = wr_ref[...]
    w_hi = w.astype(BF16)
    w_lo = (w - w_hi.astype(F32)).astype(BF16)
    logits = _dot(h, w_hi) + _dot(h, w_lo)
    lane = lax.broadcasted_iota(jnp.int32, (seq, LANES), 1)
    logits = jnp.where(lane < N_EXPERTS, logits, NEG_INF)
    ex = jnp.exp(logits - logits.max(axis=-1, keepdims=True))
    aff = ex / ex.sum(axis=-1, keepdims=True)

    def count(mask):
        ones = jnp.where(mask, 1.0, 0.0).reshape(COUNT_SPLIT, seq // COUNT_SPLIT, LANES)
        return jnp.sum(jnp.sum(ones, axis=0), axis=0, keepdims=True)

    def refine(i, ans):
        cand = ans | jnp.left_shift(jnp.int32(1), 30 - i)
        return jnp.where(count(aff >= pltpu.bitcast(cand, F32)) >= cap, cand, ans)

    thr = lax.fori_loop(0, 31, refine, jnp.zeros((1, LANES), jnp.int32))
    ge = aff >= pltpu.bitcast(thr, F32)
    gt = aff >= pltpu.bitcast(thr + 1, F32)
    gt_f = jnp.where(gt, 1.0, 0.0)
    eq_f = jnp.where(ge, 1.0, 0.0) - gt_f
    need = cap - count(gt)
    tri = jnp.where(lax.broadcasted_iota(jnp.int32, (seq, seq), 1)
                    < lax.broadcasted_iota(jnp.int32, (seq, seq), 0), 1.0, 0.0).astype(BF16)
    eq_before = _dot(tri, eq_f.astype(BF16))
    sel = gt | (ge & (eq_before < need))
    slot = _dot(tri, jnp.where(sel, 1.0, 0.0).astype(BF16))
    tgt = jnp.where(sel, slot, UNSELECTED)
    tgt_t = tgt.T
    aff_t = aff.T

    sub = lax.broadcasted_iota(jnp.int32, (cap, seq), 0).astype(F32)
    group = max(1, 256 // cap)
    for e0 in range(0, N_EXPERTS, group):
        hits = [sub == tgt_t[e:e + 1, :] for e in range(e0, e0 + group)]
        for k, hit in enumerate(hits):
            gate_ref[e0 + k] = jnp.sum(jnp.where(hit, aff_t[e0 + k:e0 + k + 1, :], 0.0), axis=1, keepdims=True)
        onehot = jnp.concatenate([jnp.where(hit, 1.0, 0.0) for hit in hits], axis=0).astype(BF16)
        rows = _dot(onehot, h).astype(BF16)
        for k in range(group):
            xe_ref[e0 + k] = rows[k * cap:(k + 1) * cap]

    lane_f = lane.astype(F32)
    for j in range(N_EXPERTS * cap // LANES):
        chunk = jnp.zeros((seq, LANES), F32)
        for e in range(j * LANES // cap, ((j + 1) * LANES - 1) // cap + 1):
            chunk = jnp.where(lane_f == tgt[:, e:e + 1] + float(e * cap - j * LANES), 1.0, chunk)
        pt_ref[0, :, j * LANES:(j + 1) * LANES] = chunk.astype(BF16)


def _route_gather(h, w_router_pad, seq, n_batch, row0, slot0, n_slots, prev=None):
    d = h.shape[1]
    cap = EC_FACTOR * seq // N_EXPERTS
    hb0 = row0 // seq
    sb0 = slot0 // cap
    in_specs = [pl.BlockSpec((seq, d), lambda b: (hb0 + b, 0)),
                pl.BlockSpec((d, LANES), lambda b: (0, 0))]
    args = [h, w_router_pad]
    aliases = {}
    if prev is not None:
        in_specs += [pl.BlockSpec(memory_space=pl.ANY), pl.BlockSpec(memory_space=pl.ANY)]
        args += list(prev)
        aliases = {2: 0, 3: 1}
    return pl.pallas_call(
        functools.partial(_route_kernel, seq=seq, cap=cap, aliased=prev is not None),
        grid=(n_batch,),
        in_specs=in_specs,
        out_specs=[pl.BlockSpec((N_EXPERTS, cap, d), lambda b: (0, sb0 + b, 0)),
                   pl.BlockSpec((N_EXPERTS, cap, 1), lambda b: (0, sb0 + b, 0)),
                   pl.BlockSpec((1, seq, N_EXPERTS * cap), lambda b: (b, 0, 0))],
        out_shape=[jax.ShapeDtypeStruct((N_EXPERTS, n_slots, d), BF16),
                   jax.ShapeDtypeStruct((N_EXPERTS, n_slots, 1), F32),
                   jax.ShapeDtypeStruct((n_batch, seq, N_EXPERTS * cap), BF16)],
        input_output_aliases=aliases,
        compiler_params=_params("arbitrary"),
        name="route_gather",
    )(*args)


FFN_ROWS = 256


def _ffn_up_kernel(x_ref, wg_ref, wu_ref, o_ref, wgb_ref, wub_ref):
    wgb_ref[...] = wg_ref[0].astype(BF16)
    wub_ref[...] = wu_ref[0].astype(BF16)
    for m in range(x_ref.shape[1] // FFN_ROWS):
        rows = slice(m * FFN_ROWS, (m + 1) * FFN_ROWS)
        x = x_ref[0, rows, :]
        o_ref[0, rows, :] = (_silu(_dot(x, wgb_ref[...])) * _dot(x, wub_ref[...])).astype(o_ref.dtype)


def _ffn_up(xe, w_gate, w_up, layer, tf=512):
    ne, r, d = xe.shape
    f = w_gate.shape[3]
    return pl.pallas_call(
        _ffn_up_kernel,
        grid=(ne, f // tf),
        in_specs=[pl.BlockSpec((1, r, d), lambda e, j: (e, 0, 0)),
                  pl.BlockSpec((None, 1, d, tf), lambda e, j: (layer, e, 0, j)),
                  pl.BlockSpec((None, 1, d, tf), lambda e, j: (layer, e, 0, j))],
        out_specs=pl.BlockSpec((1, r, tf), lambda e, j: (e, 0, j)),
        out_shape=jax.ShapeDtypeStruct((ne, r, f), BF16),
        scratch_shapes=[pltpu.VMEM((d, tf), BF16), pltpu.VMEM((d, tf), BF16)],
        compiler_params=_params("arbitrary", "arbitrary"),
        name="ffn_up",
    )(xe, w_gate, w_up)


def _ffn_down_kernel(h_ref, wd_ref, g_ref, o_ref, wdb_ref):
    wdb_ref[...] = wd_ref[0].astype(BF16)
    for m in range(h_ref.shape[1] // FFN_ROWS):
        rows = slice(m * FFN_ROWS, (m + 1) * FFN_ROWS)
        o_ref[0, rows, :] = (_dot(h_ref[0, rows, :], wdb_ref[...]) * g_ref[0, rows, :]).astype(o_ref.dtype)


def _ffn_down(hmid, w_down, gate, layer, tn=1024):
    ne, r, f = hmid.shape
    d = w_down.shape[3]
    return pl.pallas_call(
        _ffn_down_kernel,
        grid=(ne, d // tn),
        in_specs=[pl.BlockSpec((1, r, f), lambda e, j: (e, 0, 0)),
                  pl.BlockSpec((None, 1, f, tn), lambda e, j: (layer, e, 0, j)),
                  pl.BlockSpec((1, r, 1), lambda e, j: (e, 0, 0))],
        out_specs=pl.BlockSpec((1, r, tn), lambda e, j: (e, 0, j)),
        out_shape=jax.ShapeDtypeStruct((ne, r, d), BF16),
        scratch_shapes=[pltpu.VMEM((f, tn), BF16)],
        compiler_params=_params("arbitrary", "arbitrary"),
        name="ffn_down",
    )(hmid, w_down, gate)


COMBINE_ROWS = 512


def _moe_tokens(pt_ref, y_ref, x_ref, g_ref):
    ne, cap, d = y_ref.shape
    return x_ref[...] + g_ref[0] * _dot(pt_ref[0], y_ref[...].reshape(ne * cap, d))


def _combine_next_kernel(pt_ref, y_ref, x_ref, g_ref, ng_ref, sh_ref, sc_ref, *rest):
    o_ref, h_ref = rest[-2:]
    xn = _moe_tokens(pt_ref, y_ref, x_ref, g_ref)
    o_ref[...] = xn
    h_ref[...] = _norm_modulate(xn, ng_ref[0], sh_ref[0], sc_ref[0]).astype(h_ref.dtype)


def _combine_final_kernel(pt_ref, y_ref, x_ref, g_ref, ng_ref, o_ref):
    xn = _moe_tokens(pt_ref, y_ref, x_ref, g_ref)
    o_ref[...] = xn * lax.rsqrt(jnp.mean(xn * xn, axis=-1, keepdims=True) + NORM_EPS) * ng_ref[...]


def _combine(pt, y, x, mod, layer, seq, row0, slot0, n_prompt_rows, nxt, h_prev=None):
    n_batch = pt.shape[0]
    d = x.shape[1]
    cap = EC_FACTOR * seq // N_EXPERTS
    tr = min(seq, COMBINE_ROWS)
    per_seq = seq // tr
    xt0 = row0 // tr
    sb0 = slot0 // cap

    def tile(b, t):
        return xt0 + b * per_seq + t

    def mod_spec(lyr, which):
        return pl.BlockSpec(
            (1, 1, d), lambda b, t: ((lyr * COND_ROWS + _cond_of_tile(tile(b, t), tr, n_prompt_rows)) * 6 + which,
                                     0, 0))

    in_specs = [pl.BlockSpec((1, tr, N_EXPERTS * cap), lambda b, t: (b, t, 0)),
                pl.BlockSpec((N_EXPERTS, cap, d), lambda b, t: (0, sb0 + b, 0)),
                pl.BlockSpec((tr, d), lambda b, t: (tile(b, t), 0)),
                mod_spec(layer, 5)]
    if isinstance(nxt, tuple):
        g_next, l_next = nxt
        in_specs += [pl.BlockSpec((1, 1, d), lambda b, t: (l_next, 0, 0)), mod_spec(l_next, 0), mod_spec(l_next, 1)]
        args = [pt, y, x, mod, g_next, mod, mod]
        aliases = {2: 0}
        if h_prev is not None:
            in_specs.append(pl.BlockSpec(memory_space=pl.ANY))
            args.append(h_prev)
            aliases[7] = 1
        return pl.pallas_call(
            _combine_next_kernel,
            grid=(n_batch, per_seq),
            in_specs=in_specs,
            out_specs=[pl.BlockSpec((tr, d), lambda b, t: (tile(b, t), 0)),
                       pl.BlockSpec((tr, d), lambda b, t: (tile(b, t), 0))],
            out_shape=[jax.ShapeDtypeStruct(x.shape, x.dtype), jax.ShapeDtypeStruct(x.shape, BF16)],
            input_output_aliases=aliases,
            compiler_params=_params("arbitrary", "arbitrary"),
            name="moe_combine",
        )(*args)
    in_specs.append(pl.BlockSpec((1, d), lambda b, t: (0, 0)))
    return pl.pallas_call(
        _combine_final_kernel,
        grid=(n_batch, per_seq),
        in_specs=in_specs,
        out_specs=pl.BlockSpec((tr, d), lambda b, t: (b * per_seq + t, 0)),
        out_shape=jax.ShapeDtypeStruct((n_batch * seq, d), F32),
        compiler_params=_params("arbitrary", "arbitrary"),
        name="moe_combine_final",
    )(pt, y, x, mod, nxt)


def _moe(h, x, mod, layer, w_router, w_gate, w_up, w_down, n_prompt, n_latent, nxt):
    n_prompt_rows = n_prompt * SEQ
    cap_p = EC_FACTOR * SEQ // N_EXPERTS
    cap_s = EC_FACTOR * DEC_SEQ // N_EXPERTS
    slots_p = n_prompt * cap_p
    n_slots = slots_p + n_latent * cap_s
    wr = jnp.pad(w_router[layer], ((0, 0), (0, LANES - N_EXPERTS)))
    xe, gate, pt_p = _route_gather(h, wr, SEQ, n_prompt, 0, 0, n_slots)
    xe, gate, pt_s = _route_gather(h, wr, DEC_SEQ, n_latent, n_prompt_rows, slots_p, n_slots, prev=(xe, gate))
    y = _ffn_down(_ffn_up(xe, w_gate, w_up, layer), w_down, gate, layer)
    if isinstance(nxt, tuple):
        x, h = _combine(pt_p, y, x, mod, layer, SEQ, 0, 0, n_prompt_rows, nxt)
        return _combine(pt_s, y, x, mod, layer, DEC_SEQ, n_prompt_rows, slots_p, n_prompt_rows, nxt, h_prev=h)
    return (_combine(pt_p, y, x, mod, layer, SEQ, 0, 0, n_prompt_rows, nxt),
            _combine(pt_s, y, x, mod, layer, DEC_SEQ, n_prompt_rows, slots_p, n_prompt_rows, nxt))


def kernel(x_prompt, x_sample, cache_k_0, cache_v_0, cache_k_1, cache_v_1, cache_k_3, cache_v_3, c, c_ctx, norm1_g, norm2_g, w_mod, b_mod, a_w_qkv, a_sink, a_w_o, b_w_qkv, b_rpb, b_w_o, c_w_in, c_v_g, c_w_s, c_b_s, c_w_out, w_router, w_gate, w_up, w_down, final_g):
    n_prompt, _, d = x_prompt.shape
    n_latent = x_sample.shape[0]
    depth = w_mod.shape[0]
    p_rows = n_prompt * SEQ
    s_rows = n_latent * DEC_SEQ
    caches = {0: (cache_k_0, cache_v_0), 1: (cache_k_1, cache_v_1), 3: (cache_k_3, cache_v_3)}

    cond = jnp.concatenate([c_ctx[None], c, jnp.zeros((COND_ROWS - 1 - n_latent, d), F32)], axis=0)
    mod = _adaln_all(cond, w_mod, b_mod).reshape(depth * COND_ROWS * 6, 1, d)
    g1 = norm1_g.reshape(depth, 1, d)
    g2 = norm2_g.reshape(depth, 1, d)
    stream = _enter_stream(x_prompt.reshape(p_rows, d), 0, p_rows + s_rows, g1, mod, p_rows)
    x, h = _enter_stream(x_sample.reshape(s_rows, d), p_rows, p_rows + s_rows, g1, mod, p_rows, prev=stream)

    new_state = []
    for l in range(depth):
        kind, j = l % 3, l // 3
        if kind == 0:
            qkv = _matmul(h, a_w_qkv, j, F32)
            o, k_new, v_new = _attn_ctx(qkv, a_sink[j], p_rows + s_rows, n_prompt, N_KV_HEADS)
            o = _attn_win(qkv, *caches[l], a_sink[j], o, p_rows, n_latent)
            w_out = a_w_o[j]
            new_state += [k_new, v_new]
        elif kind == 1:
            qkv = _matmul(h, b_w_qkv, j, F32)
            o, k_new, v_new = _attn_ctx(qkv, None, p_rows + s_rows, n_prompt, N_HEADS)
            o = _attn_na(qkv, *caches[l], b_rpb[j], o, p_rows, n_latent)
            w_out = b_w_o[j]
            new_state += [k_new, v_new]
        else:
            z = _matmul(h, c_w_in, j, BF16, act="gelu")
            o = _sgu(z, c_v_g[j], c_w_s[j], c_b_s[j])
            w_out = c_w_out[j]
        x, h = _matmul_residual(o, w_out.astype(BF16), x, mod, g2, l, p_rows)
        if l + 1 < depth:
            x, h = _moe(h, x, mod, l, w_router, w_gate, w_up, w_down, n_prompt, n_latent, (g1, l + 1))
    y_prompt, y_sample = _moe(h, x, mod, depth - 1, w_router, w_gate, w_up, w_down, n_prompt, n_latent,
                              final_g.reshape(1, d))
    return (y_prompt.reshape(n_prompt, SEQ, d), y_sample.reshape(n_latent, DEC_SEQ, d), *new_state)
```

```python
import functools

import numpy as np
import jax
import jax.numpy as jnp
from jax import lax
from jax.experimental import pallas as pl
from jax.experimental.pallas import tpu as pltpu

F32 = jnp.float32
BF16 = jnp.bfloat16

D_MODEL = 2048
HEAD_DIM = 128
N_HEADS = D_MODEL // HEAD_DIM
N_KV_HEADS = 4
KV_REP = N_HEADS // N_KV_HEADS
ATTN_SCALE = HEAD_DIM ** -0.5
SEQ = 256
DEC_SEQ = 1024
WINDOW = 128
WIN_SPAN = 3 * WINDOW
GRID_W = 64
GRID_ROWS = DEC_SEQ // GRID_W
NA_ROWS = 8
NA_COLS = 16
ROPE_BASE = 10000.0
ROPE_FREQS = HEAD_DIM // 4
CHUNK = 128
SG_GROUPS = 16
N_EXPERTS = 16
EC_FACTOR = 2
NORM_EPS = 1e-6
NEG_INF = -1e30
COND_ROWS = 16
LANES = 128
SOFTMAX_ROWS = 64
MM_ROWS = 256
COUNT_SPLIT = 8
UNSELECTED = -4096.0
VMEM_LIMIT = 56 * 1024 * 1024


def _params(*sem):
    return pltpu.CompilerParams(dimension_semantics=sem, vmem_limit_bytes=VMEM_LIMIT)


def _dot(a, b):
    return jnp.dot(a, b, preferred_element_type=F32)


def _dot_nt(a, b):
    return lax.dot_general(a, b, (((1,), (1,)), ((), ())), preferred_element_type=F32)


def _silu(x):
    return x / (1.0 + jnp.exp(-x))


def _cond_of_tile(i, tm, n_prompt_rows):
    n_p = n_prompt_rows // tm
    per_b = DEC_SEQ // tm
    return jnp.where(i < n_p, 0, 1 + (i - n_p) // per_b)


def _adaln_kernel(c_ref, w_ref, b_ref, o_ref):
    s = _silu(c_ref[...]).astype(BF16)
    o_ref[0] = _dot(s, w_ref[0].astype(BF16)) + b_ref[0]


def _adaln_all(cond, w_mod, b_mod):
    depth, d, n = w_mod.shape
    tn = 1024
    return pl.pallas_call(
        _adaln_kernel,
        grid=(depth, n // tn),
        in_specs=[pl.BlockSpec((COND_ROWS, d), lambda l, j: (0, 0)),
                  pl.BlockSpec((1, d, tn), lambda l, j: (l, 0, j)),
                  pl.BlockSpec((1, 1, tn), lambda l, j: (l, 0, j))],
        out_specs=pl.BlockSpec((1, COND_ROWS, tn), lambda l, j: (l, 0, j)),
        out_shape=jax.ShapeDtypeStruct((depth, COND_ROWS, n), F32),
        compiler_params=_params("arbitrary", "arbitrary"),
        name="adaln",
    )(cond, w_mod, b_mod.reshape(depth, 1, n))


def _mod_spec(layer, which, tm, n_prompt_rows, row_axis, tn=D_MODEL, col_axis=None):
    def index(*ids):
        cond = _cond_of_tile(ids[row_axis], tm, n_prompt_rows)
        col = 0 if col_axis is None else ids[col_axis]
        return ((layer * COND_ROWS + cond) * 6 + which, 0, col)
    return pl.BlockSpec((1, 1, tn), index)


def _norm_modulate(x, ng, shift, scale):
    y = x * lax.rsqrt(jnp.mean(x * x, axis=-1, keepdims=True) + NORM_EPS) * ng
    return y * (1.0 + scale) + shift


def _modulate_kernel(x_ref, g_ref, sh_ref, sc_ref, *rest):
    xo_ref, h_ref = rest[-2:]
    x = x_ref[...]
    xo_ref[...] = x
    h_ref[...] = _norm_modulate(x, g_ref[0], sh_ref[0], sc_ref[0]).astype(h_ref.dtype)


def _enter_stream(x_part, row0, n_rows, g_all, mod, n_prompt_rows, prev=None):
    n, d = x_part.shape
    tm = 512
    t0 = row0 // tm
    in_specs = [pl.BlockSpec((tm, d), lambda i: (i, 0)),
                pl.BlockSpec((1, 1, d), lambda i: (0, 0, 0)),
                pl.BlockSpec((1, 1, d), lambda i: ((_cond_of_tile(t0 + i, tm, n_prompt_rows)) * 6, 0, 0)),
                pl.BlockSpec((1, 1, d), lambda i: ((_cond_of_tile(t0 + i, tm, n_prompt_rows)) * 6 + 1, 0, 0))]
    args = [x_part, g_all, mod, mod]
    aliases = {}
    if prev is not None:
        in_specs += [pl.BlockSpec(memory_space=pl.ANY), pl.BlockSpec(memory_space=pl.ANY)]
        args += list(prev)
        aliases = {4: 0, 5: 1}
    return pl.pallas_call(
        _modulate_kernel,
        grid=(n // tm,),
        in_specs=in_specs,
        out_specs=[pl.BlockSpec((tm, d), lambda i: (t0 + i, 0)),
                   pl.BlockSpec((tm, d), lambda i: (t0 + i, 0))],
        out_shape=[jax.ShapeDtypeStruct((n_rows, d), F32), jax.ShapeDtypeStruct((n_rows, d), BF16)],
        input_output_aliases=aliases,
        compiler_params=_params("arbitrary"),
        name="enter_stream",
    )(*args)


def _mm_kernel(x_ref, w_ref, o_ref, wb_ref, *, act):
    @pl.when(pl.program_id(1) == 0)
    def _():
        wb_ref[...] = w_ref[...].astype(BF16)

    for c in range(x_ref.shape[0] // MM_ROWS):
        rows = slice(c * MM_ROWS, (c + 1) * MM_ROWS)
        acc = _dot(x_ref[rows, :], wb_ref[...])
        if act == "gelu":
            acc = jax.nn.gelu(acc)
        o_ref[rows, :] = acc.astype(o_ref.dtype)


def _matmul(x, w_all, widx, out_dtype, act=None, tm=1024, tn=1024):
    m, k = x.shape
    n = w_all.shape[2]
    return pl.pallas_call(
        functools.partial(_mm_kernel, act=act),
        grid=(n // tn, m // tm),
        in_specs=[pl.BlockSpec((tm, k), lambda j, i: (i, 0)),
                  pl.BlockSpec((None, k, tn), lambda j, i: (widx, 0, j))],
        out_specs=pl.BlockSpec((tm, tn), lambda j, i: (i, j)),
        out_shape=jax.ShapeDtypeStruct((m, n), out_dtype),
        scratch_shapes=[pltpu.VMEM((k, tn), BF16)],
        compiler_params=_params("arbitrary", "arbitrary"),
        name="matmul",
    )(x, w_all)


def _mm_res_kernel(x_ref, w_ref, r_ref, g_ref, ng_ref, sh_ref, sc_ref, o_ref, h_ref):
    for c in range(x_ref.shape[0] // MM_ROWS):
        rows = slice(c * MM_ROWS, (c + 1) * MM_ROWS)
        xn = r_ref[rows, :] + g_ref[0] * _dot(x_ref[rows, :], w_ref[...])
        o_ref[rows, :] = xn
        h_ref[rows, :] = _norm_modulate(xn, ng_ref[0], sh_ref[0], sc_ref[0]).astype(h_ref.dtype)


def _matmul_residual(x, w, res, mod, g2, layer, n_prompt_rows, tm=512):
    m, k = x.shape
    n = w.shape[1]
    return pl.pallas_call(
        _mm_res_kernel,
        grid=(m // tm,),
        in_specs=[pl.BlockSpec((tm, k), lambda i: (i, 0)),
                  pl.BlockSpec((k, n), lambda i: (0, 0)),
                  pl.BlockSpec((tm, n), lambda i: (i, 0)),
                  _mod_spec(layer, 2, tm, n_prompt_rows, 0),
                  pl.BlockSpec((1, 1, n), lambda i: (layer, 0, 0)),
                  _mod_spec(layer, 3, tm, n_prompt_rows, 0),
                  _mod_spec(layer, 4, tm, n_prompt_rows, 0)],
        out_specs=[pl.BlockSpec((tm, n), lambda i: (i, 0)),
                   pl.BlockSpec((tm, n), lambda i: (i, 0))],
        out_shape=[jax.ShapeDtypeStruct((m, n), F32), jax.ShapeDtypeStruct((m, n), BF16)],
        input_output_aliases={2: 0},
        compiler_params=_params("arbitrary"),
        name="matmul_residual",
    )(x, w, res, mod, g2, mod, mod)


def _softmax_parts(parts, sink=None):
    m = parts[0].max(axis=-1, keepdims=True)
    for s in parts[1:]:
        m = jnp.maximum(m, s.max(axis=-1, keepdims=True))
    if sink is not None:
        m = jnp.maximum(m, sink)
    ps = [jnp.exp(s - m) for s in parts]
    den = ps[0].sum(axis=-1, keepdims=True)
    for p in ps[1:]:
        den = den + p.sum(axis=-1, keepdims=True)
    if sink is not None:
        den = den + jnp.exp(sink - m)
    return ps, 1.0 / den


def _attn_ctx_kernel(sink_ref, q_ref, k_ref, v_ref, o_ref, ks_ref, vs_ref, *, rep, use_sink):
    k_b = v_b = None
    for h in range(N_HEADS):
        g = h // rep
        if h % rep == 0:
            k_f = k_ref[:, g * HEAD_DIM:(g + 1) * HEAD_DIM]
            v_f = v_ref[:, g * HEAD_DIM:(g + 1) * HEAD_DIM]
            ks_ref[0, :, g, :] = k_f
            vs_ref[0, :, g, :] = v_f
            k_b = k_f.astype(BF16)
            v_b = v_f.astype(BF16)
        q = (q_ref[:, h * HEAD_DIM:(h + 1) * HEAD_DIM] * ATTN_SCALE).astype(BF16)
        s = _dot_nt(q, k_b)
        (p,), inv = _softmax_parts([s], sink_ref[h] if use_sink else None)
        o = _dot(p.astype(BF16), v_b) * inv
        o_ref[:, h * HEAD_DIM:(h + 1) * HEAD_DIM] = o.astype(o_ref.dtype)


def _attn_ctx(qkv, sink, n_rows, n_batch, n_kv):
    wq = N_HEADS * HEAD_DIM
    wk = n_kv * HEAD_DIM
    use_sink = sink is not None
    if sink is None:
        sink = jnp.zeros((N_HEADS,), F32)
    return pl.pallas_call(
        functools.partial(_attn_ctx_kernel, rep=N_HEADS // n_kv, use_sink=use_sink),
        grid=(n_batch,),
        in_specs=[pl.BlockSpec(memory_space=pltpu.SMEM),
                  pl.BlockSpec((SEQ, wq), lambda b: (b, 0)),
                  pl.BlockSpec((SEQ, wk), lambda b: (b, wq // wk)),
                  pl.BlockSpec((SEQ, wk), lambda b: (b, wq // wk + 1))],
        out_specs=[pl.BlockSpec((SEQ, wq), lambda b: (b, 0)),
                   pl.BlockSpec((1, SEQ, n_kv, HEAD_DIM), lambda b: (b, 0, 0, 0)),
                   pl.BlockSpec((1, SEQ, n_kv, HEAD_DIM), lambda b: (b, 0, 0, 0))],
        out_shape=[jax.ShapeDtypeStruct((n_rows, wq), BF16),
                   jax.ShapeDtypeStruct((n_batch, SEQ, n_kv, HEAD_DIM), F32),
                   jax.ShapeDtypeStruct((n_batch, SEQ, n_kv, HEAD_DIM), F32)],
        compiler_params=_params("arbitrary"),
        name="attn_ctx",
    )(sink, qkv, qkv, qkv)


def _rope(x, cos_t, sin_t):
    lane = lax.broadcasted_iota(jnp.int32, x.shape, 1)
    partner = jnp.where((lane & ROPE_FREQS) == 0,
                        pltpu.roll(x, HEAD_DIM - ROPE_FREQS, 1), pltpu.roll(x, ROPE_FREQS, 1))
    return x * cos_t + partner * sin_t


def _attn_win_kernel(sink_ref, q_ref, k_ref, v_ref, kc_ref, vc_ref, cq_ref, sq_ref, ck_ref, sk_ref,
                     _, o_ref, kr_ref, kcb_ref, vcb_ref):
    i = pl.program_id(1)

    @pl.when(i == 0)
    def _():
        for g in range(N_KV_HEADS):
            cols = slice(g * HEAD_DIM, (g + 1) * HEAD_DIM)
            kr_ref[:, cols] = _rope(k_ref[:, cols], ck_ref[...], sk_ref[...]).astype(BF16)
            kcb_ref[:, cols] = kc_ref[0, :, g, :].astype(BF16)
            vcb_ref[:, cols] = vc_ref[0, :, g, :].astype(BF16)

    start = pl.multiple_of(jnp.clip(i * WINDOW - WINDOW, 0, DEC_SEQ - WIN_SPAN), WINDOW)
    qpos = i * WINDOW + lax.broadcasted_iota(jnp.int32, (WINDOW, WIN_SPAN), 0)
    kpos = start + lax.broadcasted_iota(jnp.int32, (WINDOW, WIN_SPAN), 1)
    band = jnp.where(jnp.abs(qpos - kpos) <= WINDOW, 0.0, NEG_INF)
    cq, sq = cq_ref[...], sq_ref[...]
    for g in range(N_KV_HEADS):
        cols = slice(g * HEAD_DIM, (g + 1) * HEAD_DIM)
        kw = kr_ref[pl.ds(start, WIN_SPAN), cols]
        vw = v_ref[pl.ds(start, WIN_SPAN), cols].astype(BF16)
        kc = kcb_ref[:, cols]
        vc = vcb_ref[:, cols]
        heads = range(g * KV_REP, (g + 1) * KV_REP)
        qs = jnp.concatenate(
            [_rope(q_ref[:, h * HEAD_DIM:(h + 1) * HEAD_DIM], cq, sq) * ATTN_SCALE for h in heads],
            axis=0).astype(BF16)
        s_w = _dot_nt(qs, kw)
        s_c = _dot_nt(qs, kc)
        pcs, pws, invs = [], [], []
        for t in range(KV_REP * WINDOW // SOFTMAX_ROWS):
            rows = slice(t * SOFTMAX_ROWS, (t + 1) * SOFTMAX_ROWS)
            q0 = (t * SOFTMAX_ROWS) % WINDOW
            sink = sink_ref[g * KV_REP + t * SOFTMAX_ROWS // WINDOW]
            (p_c, p_w), inv = _softmax_parts([s_c[rows], s_w[rows] + band[q0:q0 + SOFTMAX_ROWS]], sink)
            pcs.append(p_c.astype(BF16))
            pws.append(p_w.astype(BF16))
            invs.append(inv)
        o = (_dot(jnp.concatenate(pcs, axis=0), vc) + _dot(jnp.concatenate(pws, axis=0), vw)
             ) * jnp.concatenate(invs, axis=0)
        for r, h in enumerate(heads):
            o_ref[:, h * HEAD_DIM:(h + 1) * HEAD_DIM] = o[r * WINDOW:(r + 1) * WINDOW].astype(o_ref.dtype)


def _rope_tables(n):
    t = np.arange(n)
    row = jnp.asarray(t // GRID_W, F32)
    col = jnp.asarray(t % GRID_W, F32)
    inv = ROPE_BASE ** (-jnp.arange(ROPE_FREQS, dtype=F32) / ROPE_FREQS)
    ar, ac = row[:, None] * inv, col[:, None] * inv
    cos_t = jnp.concatenate([jnp.cos(ar), jnp.cos(ar), jnp.cos(ac), jnp.cos(ac)], axis=1)
    sin_t = jnp.concatenate([-jnp.sin(ar), jnp.sin(ar), -jnp.sin(ac), jnp.sin(ac)], axis=1)
    return cos_t, sin_t


def _attn_win(qkv, kc, vc, sink, o_buf, n_prompt_rows, n_batch):
    wq = N_HEADS * HEAD_DIM
    wk = N_KV_HEADS * HEAD_DIM
    cos_t, sin_t = _rope_tables(DEC_SEQ)
    nqb = DEC_SEQ // WINDOW
    qb0 = n_prompt_rows // WINDOW
    sb0 = n_prompt_rows // DEC_SEQ
    past = kc.shape[1]
    return pl.pallas_call(
        _attn_win_kernel,
        grid=(n_batch, nqb),
        in_specs=[pl.BlockSpec(memory_space=pltpu.SMEM),
                  pl.BlockSpec((WINDOW, wq), lambda b, i: (qb0 + b * nqb + i, 0)),
                  pl.BlockSpec((DEC_SEQ, wk), lambda b, i: (sb0 + b, wq // wk)),
                  pl.BlockSpec((DEC_SEQ, wk), lambda b, i: (sb0 + b, wq // wk + 1)),
                  pl.BlockSpec((1, past, N_KV_HEADS, HEAD_DIM), lambda b, i: (b, 0, 0, 0)),
                  pl.BlockSpec((1, past, N_KV_HEADS, HEAD_DIM), lambda b, i: (b, 0, 0, 0)),
                  pl.BlockSpec((WINDOW, HEAD_DIM), lambda b, i: (i, 0)),
                  pl.BlockSpec((WINDOW, HEAD_DIM), lambda b, i: (i, 0)),
                  pl.BlockSpec((DEC_SEQ, HEAD_DIM), lambda b, i: (0, 0)),
                  pl.BlockSpec((DEC_SEQ, HEAD_DIM), lambda b, i: (0, 0)),
                  pl.BlockSpec(memory_space=pl.ANY)],
        out_specs=pl.BlockSpec((WINDOW, wq), lambda b, i: (qb0 + b * nqb + i, 0)),
        out_shape=jax.ShapeDtypeStruct(o_buf.shape, o_buf.dtype),
        scratch_shapes=[pltpu.VMEM((DEC_SEQ, wk), BF16), pltpu.VMEM((past, wk), BF16),
                        pltpu.VMEM((past, wk), BF16)],
        input_output_aliases={10: 0},
        compiler_params=_params("arbitrary", "arbitrary"),
        name="attn_win",
    )(sink, qkv, qkv, qkv, kc, vc, cos_t, sin_t, cos_t, sin_t, o_buf)


def _na_row_start(r):
    return min(max(r - NA_ROWS // 2, 0), GRID_ROWS - NA_ROWS)


def _attn_na_kernel(q_ref, k_ref, v_ref, kc_ref, vc_ref, bias_ref, _, o_ref):
    q = (q_ref[...] * ATTN_SCALE).astype(BF16)
    k = k_ref[...].astype(BF16)
    v = v_ref[...].astype(BF16)
    kc = kc_ref[0].astype(BF16)
    vc = vc_ref[0].astype(BF16)
    s_ctx = _dot_nt(q, kc)
    span = NA_ROWS * GRID_W
    rows = [slice(r * GRID_W, (r + 1) * GRID_W) for r in range(GRID_ROWS)]
    keys = [slice(_na_row_start(r) * GRID_W, _na_row_start(r) * GRID_W + span) for r in range(GRID_ROWS)]
    s_nbr = [_dot_nt(q[rows[r]], k[keys[r]]) + bias_ref[0, _na_row_start(r) - r + NA_ROWS - 1]
             for r in range(GRID_ROWS)]
    probs = []
    for r in range(GRID_ROWS):
        (p_c, p_n), inv = _softmax_parts([s_ctx[rows[r]], s_nbr[r]])
        probs.append((p_c.astype(BF16), p_n.astype(BF16), inv))
    for r, (p_c, p_n, inv) in enumerate(probs):
        o = (_dot(p_c, vc) + _dot(p_n, v[keys[r]])) * inv
        o_ref[rows[r], :] = o.astype(o_ref.dtype)


def _na_bias_table(rpb):
    cq = np.arange(GRID_W)[:, None]
    ck = np.arange(GRID_W)[None, :]
    cs = np.clip(cq - NA_COLS // 2, 0, GRID_W - NA_COLS)
    colmask = (ck >= cs) & (ck < cs + NA_COLS)
    dc = np.clip(ck - cq + NA_COLS - 1, 0, 2 * NA_COLS - 2)
    full = jnp.where(colmask, rpb[:, :, dc].astype(F32), NEG_INF)
    tabs = [jnp.transpose(full[:, off:off + NA_ROWS], (0, 2, 1, 3)).reshape(N_HEADS, GRID_W, NA_ROWS * GRID_W)
            for off in range(NA_ROWS)]
    return jnp.stack(tabs, axis=1)


def _attn_na(qkv, kc, vc, rpb, o_buf, n_prompt_rows, n_batch):
    sb0 = n_prompt_rows // DEC_SEQ
    kc = kc.reshape(n_batch, -1, N_HEADS * HEAD_DIM)
    vc = vc.reshape(n_batch, -1, N_HEADS * HEAD_DIM)
    past = kc.shape[1]
    bias = _na_bias_table(rpb)
    return pl.pallas_call(
        _attn_na_kernel,
        grid=(N_HEADS, n_batch),
        in_specs=[pl.BlockSpec((DEC_SEQ, HEAD_DIM), lambda h, b: (sb0 + b, h)),
                  pl.BlockSpec((DEC_SEQ, HEAD_DIM), lambda h, b: (sb0 + b, N_HEADS + h)),
                  pl.BlockSpec((DEC_SEQ, HEAD_DIM), lambda h, b: (sb0 + b, 2 * N_HEADS + h)),
                  pl.BlockSpec((1, past, HEAD_DIM), lambda h, b: (b, 0, h)),
                  pl.BlockSpec((1, past, HEAD_DIM), lambda h, b: (b, 0, h)),
                  pl.BlockSpec((1, NA_ROWS, GRID_W, NA_ROWS * GRID_W), lambda h, b: (h, 0, 0, 0)),
                  pl.BlockSpec(memory_space=pl.ANY)],
        out_specs=pl.BlockSpec((DEC_SEQ, HEAD_DIM), lambda h, b: (sb0 + b, h)),
        out_shape=jax.ShapeDtypeStruct(o_buf.shape, o_buf.dtype),
        input_output_aliases={6: 0},
        compiler_params=_params("arbitrary", "arbitrary"),
        name="attn_na",
    )(qkv, qkv, qkv, kc, vc, bias, o_buf)


SGU_CHUNKS = 2


def _sgu_kernel(z_ref, vg_ref, ws_ref, bs_ref, o_ref):
    sg = z_ref.shape[1] // 2
    for c in range(SGU_CHUNKS):
        rows = slice(c * CHUNK, (c + 1) * CHUNK)
        v = z_ref[rows, sg:].astype(F32)
        vn = v * lax.rsqrt(jnp.mean(v * v, axis=-1, keepdims=True) + NORM_EPS) * vg_ref[...]
        for g in range(SG_GROUPS):
            cols = slice(g * CHUNK, (g + 1) * CHUNK)
            f = _dot(ws_ref[g].astype(BF16), vn[:, cols].astype(BF16)) + bs_ref[:, g:g + 1]
            o_ref[rows, cols] = (z_ref[rows, cols].astype(F32) * f).astype(o_ref.dtype)


def _sgu(z, v_g, w_s, b_s):
    n, two_sg = z.shape
    sg = two_sg // 2
    tm = SGU_CHUNKS * CHUNK
    return pl.pallas_call(
        _sgu_kernel,
        grid=(n // tm,),
        in_specs=[pl.BlockSpec((tm, two_sg), lambda i: (i, 0)),
                  pl.BlockSpec((1, sg), lambda i: (0, 0)),
                  pl.BlockSpec((SG_GROUPS, CHUNK, CHUNK), lambda i: (0, 0, 0)),
                  pl.BlockSpec((CHUNK, SG_GROUPS), lambda i: (0, 0))],
        out_specs=pl.BlockSpec((tm, sg), lambda i: (i, 0)),
        out_shape=jax.ShapeDtypeStruct((n, sg), BF16),
        compiler_params=_params("arbitrary"),
        name="sgu",
    )(z, v_g.reshape(1, sg), w_s, b_s.T)


def _route_kernel(h_ref, wr_ref, *rest, seq, cap, aliased):
    if aliased:
        rest = rest[2:]
    xe_ref, gate_ref, pt_ref = rest
    h = h_ref[...]
    w = wr_ref[...]
    w_hi = w.astype(BF16)
    w_lo = (w - w_hi.astype(F32)).astype(BF16)
    both = _dot(h, jnp.concatenate([w_hi, w_lo], axis=1))
    logits = both[:, :LANES] + both[:, LANES:]
    lane = lax.broadcasted_iota(jnp.int32, (seq, LANES), 1)
    logits = jnp.where(lane < N_EXPERTS, logits, NEG_INF)
    ex = jnp.exp(logits - logits.max(axis=-1, keepdims=True))
    aff = ex / ex.sum(axis=-1, keepdims=True)

    def count(mask):
        ones = jnp.where(mask, 1.0, 0.0).reshape(COUNT_SPLIT, seq // COUNT_SPLIT, LANES)
        return jnp.sum(jnp.sum(ones, axis=0), axis=0, keepdims=True)

    def refine(i, ans):
        cand = ans | jnp.left_shift(jnp.int32(1), 30 - i)
        return jnp.where(count(aff >= pltpu.bitcast(cand, F32)) >= cap, cand, ans)

    thr = lax.fori_loop(0, 31, refine, jnp.zeros((1, LANES), jnp.int32))
    ge = aff >= pltpu.bitcast(thr, F32)
    gt = aff >= pltpu.bitcast(thr + 1, F32)
    gt_f = jnp.where(gt, 1.0, 0.0)
    eq_f = jnp.where(ge, 1.0, 0.0) - gt_f
    need = cap - count(gt)
    tri = jnp.where(lax.broadcasted_iota(jnp.int32, (seq, seq), 1)
                    < lax.broadcasted_iota(jnp.int32, (seq, seq), 0), 1.0, 0.0).astype(BF16)
    before = _dot(tri, jnp.concatenate([gt_f, eq_f], axis=1).astype(BF16))
    gt_before, eq_before = before[:, :LANES], before[:, LANES:]
    sel = gt | (ge & (eq_before < need))
    slot = gt_before + jnp.minimum(eq_before, need)
    tgt = jnp.where(sel, slot, UNSELECTED)
    tgt_t = tgt.T
    aff_t = aff.T

    sub = lax.broadcasted_iota(jnp.int32, (cap, seq), 0).astype(F32)
    group = max(1, 256 // cap)
    for e0 in range(0, N_EXPERTS, group):
        hits = [sub == tgt_t[e:e + 1, :] for e in range(e0, e0 + group)]
        for k, hit in enumerate(hits):
            gate_ref[e0 + k] = jnp.sum(jnp.where(hit, aff_t[e0 + k:e0 + k + 1, :], 0.0), axis=1, keepdims=True)
        onehot = jnp.concatenate([jnp.where(hit, 1.0, 0.0) for hit in hits], axis=0).astype(BF16)
        rows = _dot(onehot, h).astype(BF16)
        for k in range(group):
            xe_ref[e0 + k] = rows[k * cap:(k + 1) * cap]

    lane_f = lane.astype(F32)
    for j in range(N_EXPERTS * cap // LANES):
        chunk = jnp.zeros((seq, LANES), F32)
        for e in range(j * LANES // cap, ((j + 1) * LANES - 1) // cap + 1):
            chunk = jnp.where(lane_f == tgt[:, e:e + 1] + float(e * cap - j * LANES), 1.0, chunk)
        pt_ref[0, :, j * LANES:(j + 1) * LANES] = chunk.astype(BF16)


def _route_gather(h, w_router_pad, seq, n_batch, row0, slot0, n_slots, prev=None):
    d = h.shape[1]
    cap = EC_FACTOR * seq // N_EXPERTS
    hb0 = row0 // seq
    sb0 = slot0 // cap
    in_specs = [pl.BlockSpec((seq, d), lambda b: (hb0 + b, 0)),
                pl.BlockSpec((d, LANES), lambda b: (0, 0))]
    args = [h, w_router_pad]
    aliases = {}
    if prev is not None:
        in_specs += [pl.BlockSpec(memory_space=pl.ANY), pl.BlockSpec(memory_space=pl.ANY)]
        args += list(prev)
        aliases = {2: 0, 3: 1}
    return pl.pallas_call(
        functools.partial(_route_kernel, seq=seq, cap=cap, aliased=prev is not None),
        grid=(n_batch,),
        in_specs=in_specs,
        out_specs=[pl.BlockSpec((N_EXPERTS, cap, d), lambda b: (0, sb0 + b, 0)),
                   pl.BlockSpec((N_EXPERTS, cap, 1), lambda b: (0, sb0 + b, 0)),
                   pl.BlockSpec((1, seq, N_EXPERTS * cap), lambda b: (b, 0, 0))],
        out_shape=[jax.ShapeDtypeStruct((N_EXPERTS, n_slots, d), BF16),
                   jax.ShapeDtypeStruct((N_EXPERTS, n_slots, 1), F32),
                   jax.ShapeDtypeStruct((n_batch, seq, N_EXPERTS * cap), BF16)],
        input_output_aliases=aliases,
        compiler_params=_params("arbitrary"),
        name="route_gather",
    )(*args)


FFN_ROWS = 256


def _ffn_up_kernel(x_ref, wg_ref, wu_ref, o_ref, wgb_ref, wub_ref):
    wgb_ref[...] = wg_ref[0].astype(BF16)
    wub_ref[...] = wu_ref[0].astype(BF16)
    for m in range(x_ref.shape[1] // FFN_ROWS):
        rows = slice(m * FFN_ROWS, (m + 1) * FFN_ROWS)
        x = x_ref[0, rows, :]
        o_ref[0, rows, :] = (_silu(_dot(x, wgb_ref[...])) * _dot(x, wub_ref[...])).astype(o_ref.dtype)


def _ffn_up(xe, w_gate, w_up, layer, tf=512):
    ne, r, d = xe.shape
    f = w_gate.shape[3]
    return pl.pallas_call(
        _ffn_up_kernel,
        grid=(ne, f // tf),
        in_specs=[pl.BlockSpec((1, r, d), lambda e, j: (e, 0, 0)),
                  pl.BlockSpec((None, 1, d, tf), lambda e, j: (layer, e, 0, j)),
                  pl.BlockSpec((None, 1, d, tf), lambda e, j: (layer, e, 0, j))],
        out_specs=pl.BlockSpec((1, r, tf), lambda e, j: (e, 0, j)),
        out_shape=jax.ShapeDtypeStruct((ne, r, f), BF16),
        scratch_shapes=[pltpu.VMEM((d, tf), BF16), pltpu.VMEM((d, tf), BF16)],
        compiler_params=_params("arbitrary", "arbitrary"),
        name="ffn_up",
    )(xe, w_gate, w_up)


def _ffn_down_kernel(h_ref, wd_ref, g_ref, o_ref, wdb_ref):
    wdb_ref[...] = wd_ref[0].astype(BF16)
    for m in range(h_ref.shape[1] // FFN_ROWS):
        rows = slice(m * FFN_ROWS, (m + 1) * FFN_ROWS)
        o_ref[0, rows, :] = (_dot(h_ref[0, rows, :], wdb_ref[...]) * g_ref[0, rows, :]).astype(o_ref.dtype)


def _ffn_down(hmid, w_down, gate, layer, tn=2048):
    ne, r, f = hmid.shape
    d = w_down.shape[3]
    return pl.pallas_call(
        _ffn_down_kernel,
        grid=(ne, d // tn),
        in_specs=[pl.BlockSpec((1, r, f), lambda e, j: (e, 0, 0)),
                  pl.BlockSpec((None, 1, f, tn), lambda e, j: (layer, e, 0, j)),
                  pl.BlockSpec((1, r, 1), lambda e, j: (e, 0, 0))],
        out_specs=pl.BlockSpec((1, r, tn), lambda e, j: (e, 0, j)),
        out_shape=jax.ShapeDtypeStruct((ne, r, d), BF16),
        scratch_shapes=[pltpu.VMEM((f, tn), BF16)],
        compiler_params=_params("arbitrary", "arbitrary"),
        name="ffn_down",
    )(hmid, w_down, gate)


COMBINE_ROWS = 512


def _moe_tokens(pt_ref, y_ref, x_ref, g_ref):
    ne, cap, d = y_ref.shape
    return x_ref[...] + g_ref[0] * _dot(pt_ref[0], y_ref[...].reshape(ne * cap, d))


def _combine_next_kernel(pt_ref, y_ref, x_ref, g_ref, ng_ref, sh_ref, sc_ref, *rest):
    o_ref, h_ref = rest[-2:]
    xn = _moe_tokens(pt_ref, y_ref, x_ref, g_ref)
    o_ref[...] = xn
    h_ref[...] = _norm_modulate(xn, ng_ref[0], sh_ref[0], sc_ref[0]).astype(h_ref.dtype)


def _combine_final_kernel(pt_ref, y_ref, x_ref, g_ref, ng_ref, o_ref):
    xn = _moe_tokens(pt_ref, y_ref, x_ref, g_ref)
    o_ref[...] = xn * lax.rsqrt(jnp.mean(xn * xn, axis=-1, keepdims=True) + NORM_EPS) * ng_ref[...]


def _combine(pt, y, x, mod, layer, seq, row0, slot0, n_prompt_rows, nxt, h_prev=None):
    n_batch = pt.shape[0]
    d = x.shape[1]
    cap = EC_FACTOR * seq // N_EXPERTS
    tr = min(seq, COMBINE_ROWS)
    per_seq = seq // tr
    xt0 = row0 // tr
    sb0 = slot0 // cap

    def tile(b, t):
        return xt0 + b * per_seq + t

    def mod_spec(lyr, which):
        return pl.BlockSpec(
            (1, 1, d), lambda b, t: ((lyr * COND_ROWS + _cond_of_tile(tile(b, t), tr, n_prompt_rows)) * 6 + which,
                                     0, 0))

    in_specs = [pl.BlockSpec((1, tr, N_EXPERTS * cap), lambda b, t: (b, t, 0)),
                pl.BlockSpec((N_EXPERTS, cap, d), lambda b, t: (0, sb0 + b, 0)),
                pl.BlockSpec((tr, d), lambda b, t: (tile(b, t), 0)),
                mod_spec(layer, 5)]
    if isinstance(nxt, tuple):
        g_next, l_next = nxt
        in_specs += [pl.BlockSpec((1, 1, d), lambda b, t: (l_next, 0, 0)), mod_spec(l_next, 0), mod_spec(l_next, 1)]
        args = [pt, y, x, mod, g_next, mod, mod]
        aliases = {2: 0}
        if h_prev is not None:
            in_specs.append(pl.BlockSpec(memory_space=pl.ANY))
            args.append(h_prev)
            aliases[7] = 1
        return pl.pallas_call(
            _combine_next_kernel,
            grid=(n_batch, per_seq),
            in_specs=in_specs,
            out_specs=[pl.BlockSpec((tr, d), lambda b, t: (tile(b, t), 0)),
                       pl.BlockSpec((tr, d), lambda b, t: (tile(b, t), 0))],
            out_shape=[jax.ShapeDtypeStruct(x.shape, x.dtype), jax.ShapeDtypeStruct(x.shape, BF16)],
            input_output_aliases=aliases,
            compiler_params=_params("arbitrary", "arbitrary"),
            name="moe_combine",
        )(*args)
    in_specs.append(pl.BlockSpec((1, d), lambda b, t: (0, 0)))
    return pl.pallas_call(
        _combine_final_kernel,
        grid=(n_batch, per_seq),
        in_specs=in_specs,
        out_specs=pl.BlockSpec((tr, d), lambda b, t: (b * per_seq + t, 0)),
        out_shape=jax.ShapeDtypeStruct((n_batch * seq, d), F32),
        compiler_params=_params("arbitrary", "arbitrary"),
        name="moe_combine_final",
    )(pt, y, x, mod, nxt)


def _moe(h, x, mod, layer, w_router, w_gate, w_up, w_down, n_prompt, n_latent, nxt):
    n_prompt_rows = n_prompt * SEQ
    cap_p = EC_FACTOR * SEQ // N_EXPERTS
    cap_s = EC_FACTOR * DEC_SEQ // N_EXPERTS
    slots_p = n_prompt * cap_p
    n_slots = slots_p + n_latent * cap_s
    wr = jnp.pad(w_router[layer], ((0, 0), (0, LANES - N_EXPERTS)))
    xe, gate, pt_p = _route_gather(h, wr, SEQ, n_prompt, 0, 0, n_slots)
    xe, gate, pt_s = _route_gather(h, wr, DEC_SEQ, n_latent, n_prompt_rows, slots_p, n_slots, prev=(xe, gate))
    y = _ffn_down(_ffn_up(xe, w_gate, w_up, layer), w_down, gate, layer)
    if isinstance(nxt, tuple):
        x, h = _combine(pt_p, y, x, mod, layer, SEQ, 0, 0, n_prompt_rows, nxt)
        return _combine(pt_s, y, x, mod, layer, DEC_SEQ, n_prompt_rows, slots_p, n_prompt_rows, nxt, h_prev=h)
    return (_combine(pt_p, y, x, mod, layer, SEQ, 0, 0, n_prompt_rows, nxt),
            _combine(pt_s, y, x, mod, layer, DEC_SEQ, n_prompt_rows, slots_p, n_prompt_rows, nxt))


def kernel(x_prompt, x_sample, cache_k_0, cache_v_0, cache_k_1, cache_v_1, cache_k_3, cache_v_3, c, c_ctx, norm1_g, norm2_g, w_mod, b_mod, a_w_qkv, a_sink, a_w_o, b_w_qkv, b_rpb, b_w_o, c_w_in, c_v_g, c_w_s, c_b_s, c_w_out, w_router, w_gate, w_up, w_down, final_g):
    n_prompt, _, d = x_prompt.shape
    n_latent = x_sample.shape[0]
    depth = w_mod.shape[0]
    p_rows = n_prompt * SEQ
    s_rows = n_latent * DEC_SEQ
    caches = {0: (cache_k_0, cache_v_0), 1: (cache_k_1, cache_v_1), 3: (cache_k_3, cache_v_3)}

    cond = jnp.concatenate([c_ctx[None], c, jnp.zeros((COND_ROWS - 1 - n_latent, d), F32)], axis=0)
    mod = _adaln_all(cond, w_mod, b_mod).reshape(depth * COND_ROWS * 6, 1, d)
    g1 = norm1_g.reshape(depth, 1, d)
    g2 = norm2_g.reshape(depth, 1, d)
    stream = _enter_stream(x_prompt.reshape(p_rows, d), 0, p_rows + s_rows, g1, mod, p_rows)
    x, h = _enter_stream(x_sample.reshape(s_rows, d), p_rows, p_rows + s_rows, g1, mod, p_rows, prev=stream)

    new_state = []
    for l in range(depth):
        kind, j = l % 3, l // 3
        if kind == 0:
            qkv = _matmul(h, a_w_qkv, j, F32)
            o, k_new, v_new = _attn_ctx(qkv, a_sink[j], p_rows + s_rows, n_prompt, N_KV_HEADS)
            o = _attn_win(qkv, *caches[l], a_sink[j], o, p_rows, n_latent)
            w_out = a_w_o[j]
            new_state += [k_new, v_new]
        elif kind == 1:
            qkv = _matmul(h, b_w_qkv, j, F32)
            o, k_new, v_new = _attn_ctx(qkv, None, p_rows + s_rows, n_prompt, N_HEADS)
            o = _attn_na(qkv, *caches[l], b_rpb[j], o, p_rows, n_latent)
            w_out = b_w_o[j]
            new_state += [k_new, v_new]
        else:
            z = _matmul(h, c_w_in, j, BF16, act="gelu")
            o = _sgu(z, c_v_g[j], c_w_s[j], c_b_s[j])
            w_out = c_w_out[j]
        x, h = _matmul_residual(o, w_out.astype(BF16), x, mod, g2, l, p_rows)
        if l + 1 < depth:
            x, h = _moe(h, x, mod, l, w_router, w_gate, w_up, w_down, n_prompt, n_latent, (g1, l + 1))
    y_prompt, y_sample = _moe(h, x, mod, depth - 1, w_router, w_gate, w_up, w_down, n_prompt, n_latent,
                              final_g.reshape(1, d))
    return (y_prompt.reshape(n_prompt, SEQ, d), y_sample.reshape(n_latent, DEC_SEQ, d), *new_state)
```

```python
import functools

import numpy as np
import jax
import jax.numpy as jnp
from jax import lax
from jax.experimental import pallas as pl
from jax.experimental.pallas import tpu as pltpu

F32 = jnp.float32
BF16 = jnp.bfloat16

D_MODEL = 2048
HEAD_DIM = 128
N_HEADS = D_MODEL // HEAD_DIM
N_KV_HEADS = 4
KV_REP = N_HEADS // N_KV_HEADS
ATTN_SCALE = HEAD_DIM ** -0.5
SEQ = 256
DEC_SEQ = 1024
WINDOW = 128
WIN_SPAN = 3 * WINDOW
GRID_W = 64
GRID_ROWS = DEC_SEQ // GRID_W
NA_ROWS = 8
NA_COLS = 16
ROPE_BASE = 10000.0
ROPE_FREQS = HEAD_DIM // 4
CHUNK = 128
SG_GROUPS = 16
N_EXPERTS = 16
EC_FACTOR = 2
NORM_EPS = 1e-6
NEG_INF = -1e30
COND_ROWS = 16
LANES = 128
SOFTMAX_ROWS = 64
MM_ROWS = 256
COUNT_SPLIT = 8
UNSELECTED = -4096.0
VMEM_LIMIT = 56 * 1024 * 1024


def _params(*sem):
    return pltpu.CompilerParams(dimension_semantics=sem, vmem_limit_bytes=VMEM_LIMIT)


def _dot(a, b):
    return jnp.dot(a, b, preferred_element_type=F32)


def _dot_nt(a, b):
    return lax.dot_general(a, b, (((1,), (1,)), ((), ())), preferred_element_type=F32)


def _silu(x):
    return x / (1.0 + jnp.exp(-x))


def _cond_of_tile(i, tm, n_prompt_rows):
    n_p = n_prompt_rows // tm
    per_b = DEC_SEQ // tm
    return jnp.where(i < n_p, 0, 1 + (i - n_p) // per_b)


def _adaln_kernel(c_ref, w_ref, b_ref, o_ref):
    s = _silu(c_ref[...]).astype(BF16)
    o_ref[0] = _dot(s, w_ref[0].astype(BF16)) + b_ref[0]


def _adaln_all(cond, w_mod, b_mod):
    depth, d, n = w_mod.shape
    tn = 1024
    return pl.pallas_call(
        _adaln_kernel,
        grid=(depth, n // tn),
        in_specs=[pl.BlockSpec((COND_ROWS, d), lambda l, j: (0, 0)),
                  pl.BlockSpec((1, d, tn), lambda l, j: (l, 0, j)),
                  pl.BlockSpec((1, 1, tn), lambda l, j: (l, 0, j))],
        out_specs=pl.BlockSpec((1, COND_ROWS, tn), lambda l, j: (l, 0, j)),
        out_shape=jax.ShapeDtypeStruct((depth, COND_ROWS, n), F32),
        compiler_params=_params("arbitrary", "arbitrary"),
        name="adaln",
    )(cond, w_mod, b_mod.reshape(depth, 1, n))


def _mod_spec(layer, which, tm, n_prompt_rows, row_axis, tn=D_MODEL, col_axis=None):
    def index(*ids):
        cond = _cond_of_tile(ids[row_axis], tm, n_prompt_rows)
        col = 0 if col_axis is None else ids[col_axis]
        return ((layer * COND_ROWS + cond) * 6 + which, 0, col)
    return pl.BlockSpec((1, 1, tn), index)


def _norm_modulate(x, ng, shift, scale):
    y = x * lax.rsqrt(jnp.mean(x * x, axis=-1, keepdims=True) + NORM_EPS) * ng
    return y * (1.0 + scale) + shift


def _modulate_kernel(x_ref, g_ref, sh_ref, sc_ref, *rest):
    h_ref = rest[-1]
    h_ref[...] = _norm_modulate(x_ref[...], g_ref[0], sh_ref[0], sc_ref[0]).astype(h_ref.dtype)


def _enter_stream(x_part, row0, n_rows, g_all, mod, n_prompt_rows, prev=None):
    n, d = x_part.shape
    tm = 512
    t0 = row0 // tm
    in_specs = [pl.BlockSpec((tm, d), lambda i: (i, 0)),
                pl.BlockSpec((1, 1, d), lambda i: (0, 0, 0)),
                pl.BlockSpec((1, 1, d), lambda i: ((_cond_of_tile(t0 + i, tm, n_prompt_rows)) * 6, 0, 0)),
                pl.BlockSpec((1, 1, d), lambda i: ((_cond_of_tile(t0 + i, tm, n_prompt_rows)) * 6 + 1, 0, 0))]
    args = [x_part, g_all, mod, mod]
    aliases = {}
    if prev is not None:
        in_specs.append(pl.BlockSpec(memory_space=pl.ANY))
        args.append(prev)
        aliases = {4: 0}
    return pl.pallas_call(
        _modulate_kernel,
        grid=(n // tm,),
        in_specs=in_specs,
        out_specs=pl.BlockSpec((tm, d), lambda i: (t0 + i, 0)),
        out_shape=jax.ShapeDtypeStruct((n_rows, d), BF16),
        input_output_aliases=aliases,
        compiler_params=_params("arbitrary"),
        name="enter_stream",
    )(*args)


def _mm_kernel(x_ref, w_ref, o_ref, wb_ref, *, act):
    @pl.when(pl.program_id(1) == 0)
    def _():
        wb_ref[...] = w_ref[...].astype(BF16)

    for c in range(x_ref.shape[0] // MM_ROWS):
        rows = slice(c * MM_ROWS, (c + 1) * MM_ROWS)
        acc = _dot(x_ref[rows, :], wb_ref[...])
        if act == "gelu":
            acc = jax.nn.gelu(acc)
        o_ref[rows, :] = acc.astype(o_ref.dtype)


def _matmul(x, w_all, widx, out_dtype, act=None, tm=1024, tn=1024):
    m, k = x.shape
    n = w_all.shape[2]
    return pl.pallas_call(
        functools.partial(_mm_kernel, act=act),
        grid=(n // tn, m // tm),
        in_specs=[pl.BlockSpec((tm, k), lambda j, i: (i, 0)),
                  pl.BlockSpec((None, k, tn), lambda j, i: (widx, 0, j))],
        out_specs=pl.BlockSpec((tm, tn), lambda j, i: (i, j)),
        out_shape=jax.ShapeDtypeStruct((m, n), out_dtype),
        scratch_shapes=[pltpu.VMEM((k, tn), BF16)],
        compiler_params=_params("arbitrary", "arbitrary"),
        name="matmul",
    )(x, w_all)


def _mm_res_kernel(x_ref, w_ref, *rest, n_prompt_tiles):
    if n_prompt_tiles is None:
        r_ref = rest[0]
        rest = rest[1:]
    else:
        rp_ref, rs_ref = rest[:2]
        rest = rest[2:]
        from_prompt = pl.program_id(0) < n_prompt_tiles
    g_ref, ng_ref, sh_ref, sc_ref, o_ref, h_ref = rest
    for c in range(x_ref.shape[0] // MM_ROWS):
        rows = slice(c * MM_ROWS, (c + 1) * MM_ROWS)
        if n_prompt_tiles is None:
            r = r_ref[rows, :]
        else:
            r = jnp.where(from_prompt, rp_ref[rows, :], rs_ref[rows, :])
        xn = r + g_ref[0] * _dot(x_ref[rows, :], w_ref[...])
        o_ref[rows, :] = xn
        h_ref[rows, :] = _norm_modulate(xn, ng_ref[0], sh_ref[0], sc_ref[0]).astype(h_ref.dtype)


def _matmul_residual(x, w, res, mod, g2, layer, n_prompt_rows, tm=512):
    m, k = x.shape
    n = w.shape[1]
    if isinstance(res, tuple):
        pt = n_prompt_rows // tm
        res_specs = [pl.BlockSpec((tm, n), lambda i: (jnp.minimum(i, pt - 1), 0)),
                     pl.BlockSpec((tm, n), lambda i: (jnp.maximum(i - pt, 0), 0))]
        res_args, aliases = list(res), {}
    else:
        pt = None
        res_specs = [pl.BlockSpec((tm, n), lambda i: (i, 0))]
        res_args, aliases = [res], {2: 0}
    return pl.pallas_call(
        functools.partial(_mm_res_kernel, n_prompt_tiles=pt),
        grid=(m // tm,),
        in_specs=[pl.BlockSpec((tm, k), lambda i: (i, 0)),
                  pl.BlockSpec((k, n), lambda i: (0, 0)),
                  *res_specs,
                  _mod_spec(layer, 2, tm, n_prompt_rows, 0),
                  pl.BlockSpec((1, 1, n), lambda i: (layer, 0, 0)),
                  _mod_spec(layer, 3, tm, n_prompt_rows, 0),
                  _mod_spec(layer, 4, tm, n_prompt_rows, 0)],
        out_specs=[pl.BlockSpec((tm, n), lambda i: (i, 0)),
                   pl.BlockSpec((tm, n), lambda i: (i, 0))],
        out_shape=[jax.ShapeDtypeStruct((m, n), F32), jax.ShapeDtypeStruct((m, n), BF16)],
        input_output_aliases=aliases,
        compiler_params=_params("arbitrary"),
        name="matmul_residual",
    )(x, w, *res_args, mod, g2, mod, mod)


def _softmax_parts(parts, sink=None):
    m = parts[0].max(axis=-1, keepdims=True)
    for s in parts[1:]:
        m = jnp.maximum(m, s.max(axis=-1, keepdims=True))
    if sink is not None:
        m = jnp.maximum(m, sink)
    ps = [jnp.exp(s - m) for s in parts]
    den = ps[0].sum(axis=-1, keepdims=True)
    for p in ps[1:]:
        den = den + p.sum(axis=-1, keepdims=True)
    if sink is not None:
        den = den + jnp.exp(sink - m)
    return ps, 1.0 / den


def _attn_ctx_kernel(sink_ref, q_ref, k_ref, v_ref, o_ref, ks_ref, vs_ref, *, rep, use_sink):
    k_b = v_b = None
    for h in range(N_HEADS):
        g = h // rep
        if h % rep == 0:
            k_f = k_ref[:, g * HEAD_DIM:(g + 1) * HEAD_DIM]
            v_f = v_ref[:, g * HEAD_DIM:(g + 1) * HEAD_DIM]
            ks_ref[0, :, g, :] = k_f
            vs_ref[0, :, g, :] = v_f
            k_b = k_f.astype(BF16)
            v_b = v_f.astype(BF16)
        q = (q_ref[:, h * HEAD_DIM:(h + 1) * HEAD_DIM] * ATTN_SCALE).astype(BF16)
        s = _dot_nt(q, k_b)
        (p,), inv = _softmax_parts([s], sink_ref[h] if use_sink else None)
        o = _dot(p.astype(BF16), v_b) * inv
        o_ref[:, h * HEAD_DIM:(h + 1) * HEAD_DIM] = o.astype(o_ref.dtype)


def _attn_ctx(qkv, sink, n_rows, n_batch, n_kv):
    wq = N_HEADS * HEAD_DIM
    wk = n_kv * HEAD_DIM
    use_sink = sink is not None
    if sink is None:
        sink = jnp.zeros((N_HEADS,), F32)
    return pl.pallas_call(
        functools.partial(_attn_ctx_kernel, rep=N_HEADS // n_kv, use_sink=use_sink),
        grid=(n_batch,),
        in_specs=[pl.BlockSpec(memory_space=pltpu.SMEM),
                  pl.BlockSpec((SEQ, wq), lambda b: (b, 0)),
                  pl.BlockSpec((SEQ, wk), lambda b: (b, wq // wk)),
                  pl.BlockSpec((SEQ, wk), lambda b: (b, wq // wk + 1))],
        out_specs=[pl.BlockSpec((SEQ, wq), lambda b: (b, 0)),
                   pl.BlockSpec((1, SEQ, n_kv, HEAD_DIM), lambda b: (b, 0, 0, 0)),
                   pl.BlockSpec((1, SEQ, n_kv, HEAD_DIM), lambda b: (b, 0, 0, 0))],
        out_shape=[jax.ShapeDtypeStruct((n_rows, wq), BF16),
                   jax.ShapeDtypeStruct((n_batch, SEQ, n_kv, HEAD_DIM), F32),
                   jax.ShapeDtypeStruct((n_batch, SEQ, n_kv, HEAD_DIM), F32)],
        compiler_params=_params("arbitrary"),
        name="attn_ctx",
    )(sink, qkv, qkv, qkv)


def _rope(x, cos_t, sin_t):
    lane = lax.broadcasted_iota(jnp.int32, x.shape, 1)
    partner = jnp.where((lane & ROPE_FREQS) == 0,
                        pltpu.roll(x, HEAD_DIM - ROPE_FREQS, 1), pltpu.roll(x, ROPE_FREQS, 1))
    return x * cos_t + partner * sin_t


def _attn_win_kernel(sink_ref, q_ref, k_ref, v_ref, kc_ref, vc_ref, cq_ref, sq_ref, ck_ref, sk_ref,
                     _, o_ref, kr_ref, kcb_ref, vcb_ref):
    i = pl.program_id(1)

    @pl.when(i == 0)
    def _():
        for g in range(N_KV_HEADS):
            cols = slice(g * HEAD_DIM, (g + 1) * HEAD_DIM)
            kr_ref[:, cols] = _rope(k_ref[:, cols], ck_ref[...], sk_ref[...]).astype(BF16)
            kcb_ref[:, cols] = kc_ref[0, :, g, :].astype(BF16)
            vcb_ref[:, cols] = vc_ref[0, :, g, :].astype(BF16)

    start = pl.multiple_of(jnp.clip(i * WINDOW - WINDOW, 0, DEC_SEQ - WIN_SPAN), WINDOW)
    qpos = i * WINDOW + lax.broadcasted_iota(jnp.int32, (WINDOW, WIN_SPAN), 0)
    kpos = start + lax.broadcasted_iota(jnp.int32, (WINDOW, WIN_SPAN), 1)
    band = jnp.where(jnp.abs(qpos - kpos) <= WINDOW, 0.0, NEG_INF)
    cq, sq = cq_ref[...], sq_ref[...]
    for g in range(N_KV_HEADS):
        cols = slice(g * HEAD_DIM, (g + 1) * HEAD_DIM)
        kw = kr_ref[pl.ds(start, WIN_SPAN), cols]
        vw = v_ref[pl.ds(start, WIN_SPAN), cols].astype(BF16)
        kc = kcb_ref[:, cols]
        vc = vcb_ref[:, cols]
        heads = range(g * KV_REP, (g + 1) * KV_REP)
        qs = jnp.concatenate(
            [_rope(q_ref[:, h * HEAD_DIM:(h + 1) * HEAD_DIM], cq, sq) * ATTN_SCALE for h in heads],
            axis=0).astype(BF16)
        s_w = _dot_nt(qs, kw)
        s_c = _dot_nt(qs, kc)
        pcs, pws, invs = [], [], []
        for t in range(KV_REP * WINDOW // SOFTMAX_ROWS):
            rows = slice(t * SOFTMAX_ROWS, (t + 1) * SOFTMAX_ROWS)
            q0 = (t * SOFTMAX_ROWS) % WINDOW
            sink = sink_ref[g * KV_REP + t * SOFTMAX_ROWS // WINDOW]
            (p_c, p_w), inv = _softmax_parts([s_c[rows], s_w[rows] + band[q0:q0 + SOFTMAX_ROWS]], sink)
            pcs.append(p_c.astype(BF16))
            pws.append(p_w.astype(BF16))
            invs.append(inv)
        o = (_dot(jnp.concatenate(pcs, axis=0), vc) + _dot(jnp.concatenate(pws, axis=0), vw)
             ) * jnp.concatenate(invs, axis=0)
        for r, h in enumerate(heads):
            o_ref[:, h * HEAD_DIM:(h + 1) * HEAD_DIM] = o[r * WINDOW:(r + 1) * WINDOW].astype(o_ref.dtype)


def _rope_tables(n):
    t = np.arange(n)
    row = jnp.asarray(t // GRID_W, F32)
    col = jnp.asarray(t % GRID_W, F32)
    inv = ROPE_BASE ** (-jnp.arange(ROPE_FREQS, dtype=F32) / ROPE_FREQS)
    ar, ac = row[:, None] * inv, col[:, None] * inv
    cos_t = jnp.concatenate([jnp.cos(ar), jnp.cos(ar), jnp.cos(ac), jnp.cos(ac)], axis=1)
    sin_t = jnp.concatenate([-jnp.sin(ar), jnp.sin(ar), -jnp.sin(ac), jnp.sin(ac)], axis=1)
    return cos_t, sin_t


def _attn_win(qkv, kc, vc, sink, o_buf, n_prompt_rows, n_batch):
    wq = N_HEADS * HEAD_DIM
    wk = N_KV_HEADS * HEAD_DIM
    cos_t, sin_t = _rope_tables(DEC_SEQ)
    nqb = DEC_SEQ // WINDOW
    qb0 = n_prompt_rows // WINDOW
    sb0 = n_prompt_rows // DEC_SEQ
    past = kc.shape[1]
    return pl.pallas_call(
        _attn_win_kernel,
        grid=(n_batch, nqb),
        in_specs=[pl.BlockSpec(memory_space=pltpu.SMEM),
                  pl.BlockSpec((WINDOW, wq), lambda b, i: (qb0 + b * nqb + i, 0)),
                  pl.BlockSpec((DEC_SEQ, wk), lambda b, i: (sb0 + b, wq // wk)),
                  pl.BlockSpec((DEC_SEQ, wk), lambda b, i: (sb0 + b, wq // wk + 1)),
                  pl.BlockSpec((1, past, N_KV_HEADS, HEAD_DIM), lambda b, i: (b, 0, 0, 0)),
                  pl.BlockSpec((1, past, N_KV_HEADS, HEAD_DIM), lambda b, i: (b, 0, 0, 0)),
                  pl.BlockSpec((WINDOW, HEAD_DIM), lambda b, i: (i, 0)),
                  pl.BlockSpec((WINDOW, HEAD_DIM), lambda b, i: (i, 0)),
                  pl.BlockSpec((DEC_SEQ, HEAD_DIM), lambda b, i: (0, 0)),
                  pl.BlockSpec((DEC_SEQ, HEAD_DIM), lambda b, i: (0, 0)),
                  pl.BlockSpec(memory_space=pl.ANY)],
        out_specs=pl.BlockSpec((WINDOW, wq), lambda b, i: (qb0 + b * nqb + i, 0)),
        out_shape=jax.ShapeDtypeStruct(o_buf.shape, o_buf.dtype),
        scratch_shapes=[pltpu.VMEM((DEC_SEQ, wk), BF16), pltpu.VMEM((past, wk), BF16),
                        pltpu.VMEM((past, wk), BF16)],
        input_output_aliases={10: 0},
        compiler_params=_params("arbitrary", "arbitrary"),
        name="attn_win",
    )(sink, qkv, qkv, qkv, kc, vc, cos_t, sin_t, cos_t, sin_t, o_buf)


def _na_row_start(r):
    return min(max(r - NA_ROWS // 2, 0), GRID_ROWS - NA_ROWS)


def _attn_na_kernel(q_ref, k_ref, v_ref, kc_ref, vc_ref, bias_ref, _, o_ref):
    q = (q_ref[...] * ATTN_SCALE).astype(BF16)
    k = k_ref[...].astype(BF16)
    v = v_ref[...].astype(BF16)
    kc = kc_ref[0].astype(BF16)
    vc = vc_ref[0].astype(BF16)
    s_ctx = _dot_nt(q, kc)
    span = NA_ROWS * GRID_W
    rows = [slice(r * GRID_W, (r + 1) * GRID_W) for r in range(GRID_ROWS)]
    keys = [slice(_na_row_start(r) * GRID_W, _na_row_start(r) * GRID_W + span) for r in range(GRID_ROWS)]
    s_nbr = [_dot_nt(q[rows[r]], k[keys[r]]) + bias_ref[0, _na_row_start(r) - r + NA_ROWS - 1]
             for r in range(GRID_ROWS)]
    probs = []
    for r in range(GRID_ROWS):
        (p_c, p_n), inv = _softmax_parts([s_ctx[rows[r]], s_nbr[r]])
        probs.append((p_c.astype(BF16), p_n.astype(BF16), inv))
    for r, (p_c, p_n, inv) in enumerate(probs):
        o = (_dot(p_c, vc) + _dot(p_n, v[keys[r]])) * inv
        o_ref[rows[r], :] = o.astype(o_ref.dtype)


def _na_bias_table(rpb):
    cq = np.arange(GRID_W)[:, None]
    ck = np.arange(GRID_W)[None, :]
    cs = np.clip(cq - NA_COLS // 2, 0, GRID_W - NA_COLS)
    colmask = (ck >= cs) & (ck < cs + NA_COLS)
    dc = np.clip(ck - cq + NA_COLS - 1, 0, 2 * NA_COLS - 2)
    full = jnp.where(colmask, rpb[:, :, dc].astype(F32), NEG_INF)
    tabs = [jnp.transpose(full[:, off:off + NA_ROWS], (0, 2, 1, 3)).reshape(N_HEADS, GRID_W, NA_ROWS * GRID_W)
            for off in range(NA_ROWS)]
    return jnp.stack(tabs, axis=1)


def _attn_na(qkv, kc, vc, rpb, o_buf, n_prompt_rows, n_batch):
    sb0 = n_prompt_rows // DEC_SEQ
    kc = kc.reshape(n_batch, -1, N_HEADS * HEAD_DIM)
    vc = vc.reshape(n_batch, -1, N_HEADS * HEAD_DIM)
    past = kc.shape[1]
    bias = _na_bias_table(rpb)
    return pl.pallas_call(
        _attn_na_kernel,
        grid=(N_HEADS, n_batch),
        in_specs=[pl.BlockSpec((DEC_SEQ, HEAD_DIM), lambda h, b: (sb0 + b, h)),
                  pl.BlockSpec((DEC_SEQ, HEAD_DIM), lambda h, b: (sb0 + b, N_HEADS + h)),
                  pl.BlockSpec((DEC_SEQ, HEAD_DIM), lambda h, b: (sb0 + b, 2 * N_HEADS + h)),
                  pl.BlockSpec((1, past, HEAD_DIM), lambda h, b: (b, 0, h)),
                  pl.BlockSpec((1, past, HEAD_DIM), lambda h, b: (b, 0, h)),
                  pl.BlockSpec((1, NA_ROWS, GRID_W, NA_ROWS * GRID_W), lambda h, b: (h, 0, 0, 0)),
                  pl.BlockSpec(memory_space=pl.ANY)],
        out_specs=pl.BlockSpec((DEC_SEQ, HEAD_DIM), lambda h, b: (sb0 + b, h)),
        out_shape=jax.ShapeDtypeStruct(o_buf.shape, o_buf.dtype),
        input_output_aliases={6: 0},
        compiler_params=_params("arbitrary", "arbitrary"),
        name="attn_na",
    )(qkv, qkv, qkv, kc, vc, bias, o_buf)


SGU_CHUNKS = 2


def _sgu_kernel(z_ref, vg_ref, ws_ref, bs_ref, o_ref):
    sg = z_ref.shape[1] // 2
    for c in range(SGU_CHUNKS):
        rows = slice(c * CHUNK, (c + 1) * CHUNK)
        v = z_ref[rows, sg:].astype(F32)
        vn = v * lax.rsqrt(jnp.mean(v * v, axis=-1, keepdims=True) + NORM_EPS) * vg_ref[...]
        for g in range(SG_GROUPS):
            cols = slice(g * CHUNK, (g + 1) * CHUNK)
            f = _dot(ws_ref[g].astype(BF16), vn[:, cols].astype(BF16)) + bs_ref[:, g:g + 1]
            o_ref[rows, cols] = (z_ref[rows, cols].astype(F32) * f).astype(o_ref.dtype)


def _sgu(z, v_g, w_s, b_s):
    n, two_sg = z.shape
    sg = two_sg // 2
    tm = SGU_CHUNKS * CHUNK
    return pl.pallas_call(
        _sgu_kernel,
        grid=(n // tm,),
        in_specs=[pl.BlockSpec((tm, two_sg), lambda i: (i, 0)),
                  pl.BlockSpec((1, sg), lambda i: (0, 0)),
                  pl.BlockSpec((SG_GROUPS, CHUNK, CHUNK), lambda i: (0, 0, 0)),
                  pl.BlockSpec((CHUNK, SG_GROUPS), lambda i: (0, 0))],
        out_specs=pl.BlockSpec((tm, sg), lambda i: (i, 0)),
        out_shape=jax.ShapeDtypeStruct((n, sg), BF16),
        compiler_params=_params("arbitrary"),
        name="sgu",
    )(z, v_g.reshape(1, sg), w_s, b_s.T)


def _route_kernel(h_ref, wr_ref, *rest, seq, cap, aliased):
    if aliased:
        rest = rest[2:]
    xe_ref, gate_ref, pt_ref = rest
    h = h_ref[...]
    w = wr_ref[...]
    w_hi = w.astype(BF16)
    w_lo = (w - w_hi.astype(F32)).astype(BF16)
    both = _dot(h, jnp.concatenate([w_hi, w_lo], axis=1))
    logits = both[:, :LANES] + both[:, LANES:]
    lane = lax.broadcasted_iota(jnp.int32, (seq, LANES), 1)
    logits = jnp.where(lane < N_EXPERTS, logits, NEG_INF)
    ex = jnp.exp(logits - logits.max(axis=-1, keepdims=True))
    aff = ex / ex.sum(axis=-1, keepdims=True)

    def count(mask):
        ones = jnp.where(mask, 1.0, 0.0).reshape(COUNT_SPLIT, seq // COUNT_SPLIT, LANES)
        return jnp.sum(jnp.sum(ones, axis=0), axis=0, keepdims=True)

    def refine(i, ans):
        cand = ans | jnp.left_shift(jnp.int32(1), 30 - i)
        return jnp.where(count(aff >= pltpu.bitcast(cand, F32)) >= cap, cand, ans)

    thr = lax.fori_loop(0, 31, refine, jnp.zeros((1, LANES), jnp.int32))
    ge = aff >= pltpu.bitcast(thr, F32)
    gt = aff >= pltpu.bitcast(thr + 1, F32)
    gt_f = jnp.where(gt, 1.0, 0.0)
    eq_f = jnp.where(ge, 1.0, 0.0) - gt_f
    need = cap - count(gt)
    tri = jnp.where(lax.broadcasted_iota(jnp.int32, (seq, seq), 1)
                    < lax.broadcasted_iota(jnp.int32, (seq, seq), 0), 1.0, 0.0).astype(BF16)
    before = _dot(tri, jnp.concatenate([gt_f, eq_f], axis=1).astype(BF16))
    gt_before, eq_before = before[:, :LANES], before[:, LANES:]
    sel = gt | (ge & (eq_before < need))
    slot = gt_before + jnp.minimum(eq_before, need)
    tgt = jnp.where(sel, slot, UNSELECTED)
    tgt_t = tgt.T
    aff_t = aff.T

    sub = lax.broadcasted_iota(jnp.int32, (cap, seq), 0).astype(F32)
    group = max(1, 256 // cap)
    for e0 in range(0, N_EXPERTS, group):
        hits = [sub == tgt_t[e:e + 1, :] for e in range(e0, e0 + group)]
        for k, hit in enumerate(hits):
            gate_ref[e0 + k] = jnp.sum(jnp.where(hit, aff_t[e0 + k:e0 + k + 1, :], 0.0), axis=1, keepdims=True)
        onehot = jnp.concatenate([jnp.where(hit, 1.0, 0.0) for hit in hits], axis=0).astype(BF16)
        rows = _dot(onehot, h).astype(BF16)
        for k in range(group):
            xe_ref[e0 + k] = rows[k * cap:(k + 1) * cap]

    lane_f = lane.astype(F32)
    for j in range(N_EXPERTS * cap // LANES):
        chunk = jnp.zeros((seq, LANES), F32)
        for e in range(j * LANES // cap, ((j + 1) * LANES - 1) // cap + 1):
            chunk = jnp.where(lane_f == tgt[:, e:e + 1] + float(e * cap - j * LANES), 1.0, chunk)
        pt_ref[0, :, j * LANES:(j + 1) * LANES] = chunk.astype(BF16)


def _route_gather(h, w_router_pad, seq, n_batch, row0, slot0, n_slots, prev=None):
    d = h.shape[1]
    cap = EC_FACTOR * seq // N_EXPERTS
    hb0 = row0 // seq
    sb0 = slot0 // cap
    in_specs = [pl.BlockSpec((seq, d), lambda b: (hb0 + b, 0)),
                pl.BlockSpec((d, LANES), lambda b: (0, 0))]
    args = [h, w_router_pad]
    aliases = {}
    if prev is not None:
        in_specs += [pl.BlockSpec(memory_space=pl.ANY), pl.BlockSpec(memory_space=pl.ANY)]
        args += list(prev)
        aliases = {2: 0, 3: 1}
    return pl.pallas_call(
        functools.partial(_route_kernel, seq=seq, cap=cap, aliased=prev is not None),
        grid=(n_batch,),
        in_specs=in_specs,
        out_specs=[pl.BlockSpec((N_EXPERTS, cap, d), lambda b: (0, sb0 + b, 0)),
                   pl.BlockSpec((N_EXPERTS, cap, 1), lambda b: (0, sb0 + b, 0)),
                   pl.BlockSpec((1, seq, N_EXPERTS * cap), lambda b: (b, 0, 0))],
        out_shape=[jax.ShapeDtypeStruct((N_EXPERTS, n_slots, d), BF16),
                   jax.ShapeDtypeStruct((N_EXPERTS, n_slots, 1), F32),
                   jax.ShapeDtypeStruct((n_batch, seq, N_EXPERTS * cap), BF16)],
        input_output_aliases=aliases,
        compiler_params=_params("arbitrary"),
        name="route_gather",
    )(*args)


FFN_ROWS = 256


def _ffn_up_kernel(x_ref, wg_ref, wu_ref, o_ref, wgb_ref, wub_ref):
    wgb_ref[...] = wg_ref[0].astype(BF16)
    wub_ref[...] = wu_ref[0].astype(BF16)
    for m in range(x_ref.shape[1] // FFN_ROWS):
        rows = slice(m * FFN_ROWS, (m + 1) * FFN_ROWS)
        x = x_ref[0, rows, :]
        o_ref[0, rows, :] = (_silu(_dot(x, wgb_ref[...])) * _dot(x, wub_ref[...])).astype(o_ref.dtype)


def _ffn_up(xe, w_gate, w_up, layer, tf=512):
    ne, r, d = xe.shape
    f = w_gate.shape[3]
    return pl.pallas_call(
        _ffn_up_kernel,
        grid=(ne, f // tf),
        in_specs=[pl.BlockSpec((1, r, d), lambda e, j: (e, 0, 0)),
                  pl.BlockSpec((None, 1, d, tf), lambda e, j: (layer, e, 0, j)),
                  pl.BlockSpec((None, 1, d, tf), lambda e, j: (layer, e, 0, j))],
        out_specs=pl.BlockSpec((1, r, tf), lambda e, j: (e, 0, j)),
        out_shape=jax.ShapeDtypeStruct((ne, r, f), BF16),
        scratch_shapes=[pltpu.VMEM((d, tf), BF16), pltpu.VMEM((d, tf), BF16)],
        compiler_params=_params("arbitrary", "arbitrary"),
        name="ffn_up",
    )(xe, w_gate, w_up)


def _ffn_down_kernel(h_ref, wd_ref, g_ref, o_ref, wdb_ref):
    wdb_ref[...] = wd_ref[0].astype(BF16)
    for m in range(h_ref.shape[1] // FFN_ROWS):
        rows = slice(m * FFN_ROWS, (m + 1) * FFN_ROWS)
        o_ref[0, rows, :] = (_dot(h_ref[0, rows, :], wdb_ref[...]) * g_ref[0, rows, :]).astype(o_ref.dtype)


def _ffn_down(hmid, w_down, gate, layer, tn=2048):
    ne, r, f = hmid.shape
    d = w_down.shape[3]
    return pl.pallas_call(
        _ffn_down_kernel,
        grid=(ne, d // tn),
        in_specs=[pl.BlockSpec((1, r, f), lambda e, j: (e, 0, 0)),
                  pl.BlockSpec((None, 1, f, tn), lambda e, j: (layer, e, 0, j)),
                  pl.BlockSpec((1, r, 1), lambda e, j: (e, 0, 0))],
        out_specs=pl.BlockSpec((1, r, tn), lambda e, j: (e, 0, j)),
        out_shape=jax.ShapeDtypeStruct((ne, r, d), BF16),
        scratch_shapes=[pltpu.VMEM((f, tn), BF16)],
        compiler_params=_params("arbitrary", "arbitrary"),
        name="ffn_down",
    )(hmid, w_down, gate)


COMBINE_ROWS = 512


def _moe_tokens(pt_ref, y_ref, x_ref, g_ref):
    ne, cap, d = y_ref.shape
    return x_ref[...] + g_ref[0] * _dot(pt_ref[0], y_ref[...].reshape(ne * cap, d))


def _combine_next_kernel(pt_ref, y_ref, x_ref, g_ref, ng_ref, sh_ref, sc_ref, *rest):
    o_ref, h_ref = rest[-2:]
    xn = _moe_tokens(pt_ref, y_ref, x_ref, g_ref)
    o_ref[...] = xn
    h_ref[...] = _norm_modulate(xn, ng_ref[0], sh_ref[0], sc_ref[0]).astype(h_ref.dtype)


def _combine_final_kernel(pt_ref, y_ref, x_ref, g_ref, ng_ref, o_ref):
    xn = _moe_tokens(pt_ref, y_ref, x_ref, g_ref)
    o_ref[...] = xn * lax.rsqrt(jnp.mean(xn * xn, axis=-1, keepdims=True) + NORM_EPS) * ng_ref[...]


def _combine(pt, y, x, mod, layer, seq, row0, slot0, n_prompt_rows, nxt, h_prev=None):
    n_batch = pt.shape[0]
    d = x.shape[1]
    cap = EC_FACTOR * seq // N_EXPERTS
    tr = min(seq, COMBINE_ROWS)
    per_seq = seq // tr
    xt0 = row0 // tr
    sb0 = slot0 // cap

    def tile(b, t):
        return xt0 + b * per_seq + t

    def mod_spec(lyr, which):
        return pl.BlockSpec(
            (1, 1, d), lambda b, t: ((lyr * COND_ROWS + _cond_of_tile(tile(b, t), tr, n_prompt_rows)) * 6 + which,
                                     0, 0))

    in_specs = [pl.BlockSpec((1, tr, N_EXPERTS * cap), lambda b, t: (b, t, 0)),
                pl.BlockSpec((N_EXPERTS, cap, d), lambda b, t: (0, sb0 + b, 0)),
                pl.BlockSpec((tr, d), lambda b, t: (tile(b, t), 0)),
                mod_spec(layer, 5)]
    if isinstance(nxt, tuple):
        g_next, l_next = nxt
        in_specs += [pl.BlockSpec((1, 1, d), lambda b, t: (l_next, 0, 0)), mod_spec(l_next, 0), mod_spec(l_next, 1)]
        args = [pt, y, x, mod, g_next, mod, mod]
        aliases = {2: 0}
        if h_prev is not None:
            in_specs.append(pl.BlockSpec(memory_space=pl.ANY))
            args.append(h_prev)
            aliases[7] = 1
        return pl.pallas_call(
            _combine_next_kernel,
            grid=(n_batch, per_seq),
            in_specs=in_specs,
            out_specs=[pl.BlockSpec((tr, d), lambda b, t: (tile(b, t), 0)),
                       pl.BlockSpec((tr, d), lambda b, t: (tile(b, t), 0))],
            out_shape=[jax.ShapeDtypeStruct(x.shape, x.dtype), jax.ShapeDtypeStruct(x.shape, BF16)],
            input_output_aliases=aliases,
            compiler_params=_params("arbitrary", "arbitrary"),
            name="moe_combine",
        )(*args)
    in_specs.append(pl.BlockSpec((1, d), lambda b, t: (0, 0)))
    return pl.pallas_call(
        _combine_final_kernel,
        grid=(n_batch, per_seq),
        in_specs=in_specs,
        out_specs=pl.BlockSpec((tr, d), lambda b, t: (b * per_seq + t, 0)),
        out_shape=jax.ShapeDtypeStruct((n_batch * seq, d), F32),
        compiler_params=_params("arbitrary", "arbitrary"),
        name="moe_combine_final",
    )(pt, y, x, mod, nxt)


def _moe(h, x, mod, layer, w_router, w_gate, w_up, w_down, n_prompt, n_latent, nxt):
    n_prompt_rows = n_prompt * SEQ
    cap_p = EC_FACTOR * SEQ // N_EXPERTS
    cap_s = EC_FACTOR * DEC_SEQ // N_EXPERTS
    slots_p = n_prompt * cap_p
    n_slots = slots_p + n_latent * cap_s
    wr = jnp.pad(w_router[layer], ((0, 0), (0, LANES - N_EXPERTS)))
    xe, gate, pt_p = _route_gather(h, wr, SEQ, n_prompt, 0, 0, n_slots)
    xe, gate, pt_s = _route_gather(h, wr, DEC_SEQ, n_latent, n_prompt_rows, slots_p, n_slots, prev=(xe, gate))
    y = _ffn_down(_ffn_up(xe, w_gate, w_up, layer), w_down, gate, layer)
    if isinstance(nxt, tuple):
        x, h = _combine(pt_p, y, x, mod, layer, SEQ, 0, 0, n_prompt_rows, nxt)
        return _combine(pt_s, y, x, mod, layer, DEC_SEQ, n_prompt_rows, slots_p, n_prompt_rows, nxt, h_prev=h)
    return (_combine(pt_p, y, x, mod, layer, SEQ, 0, 0, n_prompt_rows, nxt),
            _combine(pt_s, y, x, mod, layer, DEC_SEQ, n_prompt_rows, slots_p, n_prompt_rows, nxt))


def kernel(x_prompt, x_sample, cache_k_0, cache_v_0, cache_k_1, cache_v_1, cache_k_3, cache_v_3, c, c_ctx, norm1_g, norm2_g, w_mod, b_mod, a_w_qkv, a_sink, a_w_o, b_w_qkv, b_rpb, b_w_o, c_w_in, c_v_g, c_w_s, c_b_s, c_w_out, w_router, w_gate, w_up, w_down, final_g):
    n_prompt, _, d = x_prompt.shape
    n_latent = x_sample.shape[0]
    depth = w_mod.shape[0]
    p_rows = n_prompt * SEQ
    s_rows = n_latent * DEC_SEQ
    caches = {0: (cache_k_0, cache_v_0), 1: (cache_k_1, cache_v_1), 3: (cache_k_3, cache_v_3)}

    cond = jnp.concatenate([c_ctx[None], c, jnp.zeros((COND_ROWS - 1 - n_latent, d), F32)], axis=0)
    mod = _adaln_all(cond, w_mod, b_mod).reshape(depth * COND_ROWS * 6, 1, d)
    g1 = norm1_g.reshape(depth, 1, d)
    g2 = norm2_g.reshape(depth, 1, d)
    x = (x_prompt.reshape(p_rows, d), x_sample.reshape(s_rows, d))
    h = _enter_stream(x[0], 0, p_rows + s_rows, g1, mod, p_rows)
    h = _enter_stream(x[1], p_rows, p_rows + s_rows, g1, mod, p_rows, prev=h)

    new_state = []
    for l in range(depth):
        kind, j = l % 3, l // 3
        if kind == 0:
            qkv = _matmul(h, a_w_qkv, j, F32)
            o, k_new, v_new = _attn_ctx(qkv, a_sink[j], p_rows + s_rows, n_prompt, N_KV_HEADS)
            o = _attn_win(qkv, *caches[l], a_sink[j], o, p_rows, n_latent)
            w_out = a_w_o[j]
            new_state += [k_new, v_new]
        elif kind == 1:
            qkv = _matmul(h, b_w_qkv, j, F32)
            o, k_new, v_new = _attn_ctx(qkv, None, p_rows + s_rows, n_prompt, N_HEADS)
            o = _attn_na(qkv, *caches[l], b_rpb[j], o, p_rows, n_latent)
            w_out = b_w_o[j]
            new_state += [k_new, v_new]
        else:
            z = _matmul(h, c_w_in, j, BF16, act="gelu")
            o = _sgu(z, c_v_g[j], c_w_s[j], c_b_s[j])
            w_out = c_w_out[j]
        x, h = _matmul_residual(o, w_out.astype(BF16), x, mod, g2, l, p_rows)
        if l + 1 < depth:
            x, h = _moe(h, x, mod, l, w_router, w_gate, w_up, w_down, n_prompt, n_latent, (g1, l + 1))
    y_prompt, y_sample = _moe(h, x, mod, depth - 1, w_router, w_gate, w_up, w_down, n_prompt, n_latent,
                              final_g.reshape(1, d))
    return (y_prompt.reshape(n_prompt, SEQ, d), y_sample.reshape(n_latent, DEC_SEQ, d), *new_state)
```

```python
import functools

import numpy as np
import jax
import jax.numpy as jnp
from jax import lax
from jax.experimental import pallas as pl
from jax.experimental.pallas import tpu as pltpu

F32 = jnp.float32
BF16 = jnp.bfloat16

D_MODEL = 2048
HEAD_DIM = 128
N_HEADS = D_MODEL // HEAD_DIM
N_KV_HEADS = 4
KV_REP = N_HEADS // N_KV_HEADS
ATTN_SCALE = HEAD_DIM ** -0.5
SEQ = 256
DEC_SEQ = 1024
WINDOW = 128
WIN_SPAN = 3 * WINDOW
GRID_W = 64
GRID_ROWS = DEC_SEQ // GRID_W
NA_ROWS = 8
NA_COLS = 16
ROPE_BASE = 10000.0
ROPE_FREQS = HEAD_DIM // 4
CHUNK = 128
SG_GROUPS = 16
N_EXPERTS = 16
EC_FACTOR = 2
NORM_EPS = 1e-6
NEG_INF = -1e30
COND_ROWS = 16
LANES = 128
SOFTMAX_ROWS = 64
MM_ROWS = 256
COUNT_SPLIT = 8
SMALLEST_NORMAL_BITS = 0x00800000
UNSELECTED = -4096.0
VMEM_LIMIT = 56 * 1024 * 1024


def _params(*sem):
    return pltpu.CompilerParams(dimension_semantics=sem, vmem_limit_bytes=VMEM_LIMIT)


def _dot(a, b):
    return jnp.dot(a, b, preferred_element_type=F32)


def _dot_nt(a, b):
    return lax.dot_general(a, b, (((1,), (1,)), ((), ())), preferred_element_type=F32)


def _silu(x):
    return x / (1.0 + jnp.exp(-x))


def _cond_of_tile(i, tm, n_prompt_rows):
    n_p = n_prompt_rows // tm
    per_b = DEC_SEQ // tm
    return jnp.where(i < n_p, 0, 1 + (i - n_p) // per_b)


def _adaln_kernel(c_ref, w_ref, b_ref, o_ref):
    s = _silu(c_ref[...]).astype(BF16)
    o_ref[0] = _dot(s, w_ref[0].astype(BF16)) + b_ref[0]


def _adaln_all(cond, w_mod, b_mod):
    depth, d, n = w_mod.shape
    tn = 1024
    return pl.pallas_call(
        _adaln_kernel,
        grid=(depth, n // tn),
        in_specs=[pl.BlockSpec((COND_ROWS, d), lambda l, j: (0, 0)),
                  pl.BlockSpec((1, d, tn), lambda l, j: (l, 0, j)),
                  pl.BlockSpec((1, 1, tn), lambda l, j: (l, 0, j))],
        out_specs=pl.BlockSpec((1, COND_ROWS, tn), lambda l, j: (l, 0, j)),
        out_shape=jax.ShapeDtypeStruct((depth, COND_ROWS, n), F32),
        compiler_params=_params("arbitrary", "arbitrary"),
        name="adaln",
    )(cond, w_mod, b_mod.reshape(depth, 1, n))


def _mod_spec(layer, which, tm, n_prompt_rows, row_axis, tn=D_MODEL, col_axis=None):
    def index(*ids):
        cond = _cond_of_tile(ids[row_axis], tm, n_prompt_rows)
        col = 0 if col_axis is None else ids[col_axis]
        return ((layer * COND_ROWS + cond) * 6 + which, 0, col)
    return pl.BlockSpec((1, 1, tn), index)


def _norm_modulate(x, ng, shift, scale):
    y = x * lax.rsqrt(jnp.mean(x * x, axis=-1, keepdims=True) + NORM_EPS) * ng
    return y * (1.0 + scale) + shift


def _modulate_kernel(x_ref, g_ref, sh_ref, sc_ref, *rest):
    h_ref = rest[-1]
    h_ref[...] = _norm_modulate(x_ref[...], g_ref[0], sh_ref[0], sc_ref[0]).astype(h_ref.dtype)


def _enter_stream(x_part, row0, n_rows, g_all, mod, n_prompt_rows, prev=None):
    n, d = x_part.shape
    tm = 512
    t0 = row0 // tm
    in_specs = [pl.BlockSpec((tm, d), lambda i: (i, 0)),
                pl.BlockSpec((1, 1, d), lambda i: (0, 0, 0)),
                pl.BlockSpec((1, 1, d), lambda i: ((_cond_of_tile(t0 + i, tm, n_prompt_rows)) * 6, 0, 0)),
                pl.BlockSpec((1, 1, d), lambda i: ((_cond_of_tile(t0 + i, tm, n_prompt_rows)) * 6 + 1, 0, 0))]
    args = [x_part, g_all, mod, mod]
    aliases = {}
    if prev is not None:
        in_specs.append(pl.BlockSpec(memory_space=pl.ANY))
        args.append(prev)
        aliases = {4: 0}
    return pl.pallas_call(
        _modulate_kernel,
        grid=(n // tm,),
        in_specs=in_specs,
        out_specs=pl.BlockSpec((tm, d), lambda i: (t0 + i, 0)),
        out_shape=jax.ShapeDtypeStruct((n_rows, d), BF16),
        input_output_aliases=aliases,
        compiler_params=_params("arbitrary"),
        name="enter_stream",
    )(*args)


def _mm_kernel(x_ref, w_ref, o_ref, wb_ref, *, act):
    @pl.when(pl.program_id(1) == 0)
    def _():
        wb_ref[...] = w_ref[...].astype(BF16)

    for c in range(x_ref.shape[0] // MM_ROWS):
        rows = slice(c * MM_ROWS, (c + 1) * MM_ROWS)
        acc = _dot(x_ref[rows, :], wb_ref[...])
        if act == "gelu":
            acc = jax.nn.gelu(acc)
        o_ref[rows, :] = acc.astype(o_ref.dtype)


def _matmul(x, w_all, widx, out_dtype, act=None, tm=1024, tn=1024):
    m, k = x.shape
    n = w_all.shape[2]
    return pl.pallas_call(
        functools.partial(_mm_kernel, act=act),
        grid=(n // tn, m // tm),
        in_specs=[pl.BlockSpec((tm, k), lambda j, i: (i, 0)),
                  pl.BlockSpec((None, k, tn), lambda j, i: (widx, 0, j))],
        out_specs=pl.BlockSpec((tm, tn), lambda j, i: (i, j)),
        out_shape=jax.ShapeDtypeStruct((m, n), out_dtype),
        scratch_shapes=[pltpu.VMEM((k, tn), BF16)],
        compiler_params=_params("arbitrary", "arbitrary"),
        name="matmul",
    )(x, w_all)


def _mm_res_kernel(x_ref, w_ref, *rest, n_prompt_tiles):
    if n_prompt_tiles is None:
        r_ref = rest[0]
        rest = rest[1:]
    else:
        rp_ref, rs_ref = rest[:2]
        rest = rest[2:]
        from_prompt = pl.program_id(0) < n_prompt_tiles
    g_ref, ng_ref, sh_ref, sc_ref, o_ref, h_ref = rest
    for c in range(x_ref.shape[0] // MM_ROWS):
        rows = slice(c * MM_ROWS, (c + 1) * MM_ROWS)
        if n_prompt_tiles is None:
            r = r_ref[rows, :]
        else:
            r = jnp.where(from_prompt, rp_ref[rows, :], rs_ref[rows, :])
        xn = r + g_ref[0] * _dot(x_ref[rows, :], w_ref[...])
        o_ref[rows, :] = xn
        h_ref[rows, :] = _norm_modulate(xn, ng_ref[0], sh_ref[0], sc_ref[0]).astype(h_ref.dtype)


def _matmul_residual(x, w, res, mod, g2, layer, n_prompt_rows, tm=512):
    m, k = x.shape
    n = w.shape[1]
    if isinstance(res, tuple):
        pt = n_prompt_rows // tm
        res_specs = [pl.BlockSpec((tm, n), lambda i: (jnp.minimum(i, pt - 1), 0)),
                     pl.BlockSpec((tm, n), lambda i: (jnp.maximum(i - pt, 0), 0))]
        res_args, aliases = list(res), {}
    else:
        pt = None
        res_specs = [pl.BlockSpec((tm, n), lambda i: (i, 0))]
        res_args, aliases = [res], {2: 0}
    return pl.pallas_call(
        functools.partial(_mm_res_kernel, n_prompt_tiles=pt),
        grid=(m // tm,),
        in_specs=[pl.BlockSpec((tm, k), lambda i: (i, 0)),
                  pl.BlockSpec((k, n), lambda i: (0, 0)),
                  *res_specs,
                  _mod_spec(layer, 2, tm, n_prompt_rows, 0),
                  pl.BlockSpec((1, 1, n), lambda i: (layer, 0, 0)),
                  _mod_spec(layer, 3, tm, n_prompt_rows, 0),
                  _mod_spec(layer, 4, tm, n_prompt_rows, 0)],
        out_specs=[pl.BlockSpec((tm, n), lambda i: (i, 0)),
                   pl.BlockSpec((tm, n), lambda i: (i, 0))],
        out_shape=[jax.ShapeDtypeStruct((m, n), F32), jax.ShapeDtypeStruct((m, n), BF16)],
        input_output_aliases=aliases,
        compiler_params=_params("arbitrary"),
        name="matmul_residual",
    )(x, w, *res_args, mod, g2, mod, mod)


def _softmax_parts(parts, sink=None):
    m = parts[0].max(axis=-1, keepdims=True)
    for s in parts[1:]:
        m = jnp.maximum(m, s.max(axis=-1, keepdims=True))
    if sink is not None:
        m = jnp.maximum(m, sink)
    ps = [jnp.exp(s - m) for s in parts]
    den = ps[0].sum(axis=-1, keepdims=True)
    for p in ps[1:]:
        den = den + p.sum(axis=-1, keepdims=True)
    if sink is not None:
        den = den + jnp.exp(sink - m)
    return ps, 1.0 / den


def _attn_ctx_kernel(sink_ref, q_ref, k_ref, v_ref, o_ref, ks_ref, vs_ref, *, rep, use_sink):
    k_b = v_b = None
    for h in range(N_HEADS):
        g = h // rep
        if h % rep == 0:
            k_f = k_ref[:, g * HEAD_DIM:(g + 1) * HEAD_DIM]
            v_f = v_ref[:, g * HEAD_DIM:(g + 1) * HEAD_DIM]
            ks_ref[0, :, g, :] = k_f
            vs_ref[0, :, g, :] = v_f
            k_b = k_f.astype(BF16)
            v_b = v_f.astype(BF16)
        q = (q_ref[:, h * HEAD_DIM:(h + 1) * HEAD_DIM] * ATTN_SCALE).astype(BF16)
        s = _dot_nt(q, k_b)
        (p,), inv = _softmax_parts([s], sink_ref[h] if use_sink else None)
        o = _dot(p.astype(BF16), v_b) * inv
        o_ref[:, h * HEAD_DIM:(h + 1) * HEAD_DIM] = o.astype(o_ref.dtype)


def _attn_ctx(qkv, sink, n_rows, n_batch, n_kv):
    wq = N_HEADS * HEAD_DIM
    wk = n_kv * HEAD_DIM
    use_sink = sink is not None
    if sink is None:
        sink = jnp.zeros((N_HEADS,), F32)
    return pl.pallas_call(
        functools.partial(_attn_ctx_kernel, rep=N_HEADS // n_kv, use_sink=use_sink),
        grid=(n_batch,),
        in_specs=[pl.BlockSpec(memory_space=pltpu.SMEM),
                  pl.BlockSpec((SEQ, wq), lambda b: (b, 0)),
                  pl.BlockSpec((SEQ, wk), lambda b: (b, wq // wk)),
                  pl.BlockSpec((SEQ, wk), lambda b: (b, wq // wk + 1))],
        out_specs=[pl.BlockSpec((SEQ, wq), lambda b: (b, 0)),
                   pl.BlockSpec((1, SEQ, n_kv, HEAD_DIM), lambda b: (b, 0, 0, 0)),
                   pl.BlockSpec((1, SEQ, n_kv, HEAD_DIM), lambda b: (b, 0, 0, 0))],
        out_shape=[jax.ShapeDtypeStruct((n_rows, wq), BF16),
                   jax.ShapeDtypeStruct((n_batch, SEQ, n_kv, HEAD_DIM), F32),
                   jax.ShapeDtypeStruct((n_batch, SEQ, n_kv, HEAD_DIM), F32)],
        compiler_params=_params("arbitrary"),
        name="attn_ctx",
    )(sink, qkv, qkv, qkv)


def _rope(x, cos_t, sin_t):
    lane = lax.broadcasted_iota(jnp.int32, x.shape, 1)
    partner = jnp.where((lane & ROPE_FREQS) == 0,
                        pltpu.roll(x, HEAD_DIM - ROPE_FREQS, 1), pltpu.roll(x, ROPE_FREQS, 1))
    return x * cos_t + partner * sin_t


def _attn_win_kernel(sink_ref, q_ref, k_ref, v_ref, kc_ref, vc_ref, cq_ref, sq_ref, ck_ref, sk_ref,
                     _, o_ref, kr_ref, kcb_ref, vcb_ref):
    i = pl.program_id(1)

    @pl.when(i == 0)
    def _():
        for g in range(N_KV_HEADS):
            cols = slice(g * HEAD_DIM, (g + 1) * HEAD_DIM)
            kr_ref[:, cols] = _rope(k_ref[:, cols], ck_ref[...], sk_ref[...]).astype(BF16)
            kcb_ref[:, cols] = kc_ref[0, :, g, :].astype(BF16)
            vcb_ref[:, cols] = vc_ref[0, :, g, :].astype(BF16)

    start = pl.multiple_of(jnp.clip(i * WINDOW - WINDOW, 0, DEC_SEQ - WIN_SPAN), WINDOW)
    qpos = i * WINDOW + lax.broadcasted_iota(jnp.int32, (WINDOW, WIN_SPAN), 0)
    kpos = start + lax.broadcasted_iota(jnp.int32, (WINDOW, WIN_SPAN), 1)
    band = jnp.where(jnp.abs(qpos - kpos) <= WINDOW, 0.0, NEG_INF)
    cq, sq = cq_ref[...], sq_ref[...]
    for g in range(N_KV_HEADS):
        cols = slice(g * HEAD_DIM, (g + 1) * HEAD_DIM)
        kw = kr_ref[pl.ds(start, WIN_SPAN), cols]
        vw = v_ref[pl.ds(start, WIN_SPAN), cols].astype(BF16)
        kc = kcb_ref[:, cols]
        vc = vcb_ref[:, cols]
        heads = range(g * KV_REP, (g + 1) * KV_REP)
        qs = jnp.concatenate(
            [_rope(q_ref[:, h * HEAD_DIM:(h + 1) * HEAD_DIM], cq, sq) * ATTN_SCALE for h in heads],
            axis=0).astype(BF16)
        s_w = _dot_nt(qs, kw)
        s_c = _dot_nt(qs, kc)
        pcs, pws, invs = [], [], []
        for t in range(KV_REP * WINDOW // SOFTMAX_ROWS):
            rows = slice(t * SOFTMAX_ROWS, (t + 1) * SOFTMAX_ROWS)
            q0 = (t * SOFTMAX_ROWS) % WINDOW
            sink = sink_ref[g * KV_REP + t * SOFTMAX_ROWS // WINDOW]
            (p_c, p_w), inv = _softmax_parts([s_c[rows], s_w[rows] + band[q0:q0 + SOFTMAX_ROWS]], sink)
            pcs.append(p_c.astype(BF16))
            pws.append(p_w.astype(BF16))
            invs.append(inv)
        o = (_dot(jnp.concatenate(pcs, axis=0), vc) + _dot(jnp.concatenate(pws, axis=0), vw)
             ) * jnp.concatenate(invs, axis=0)
        for r, h in enumerate(heads):
            o_ref[:, h * HEAD_DIM:(h + 1) * HEAD_DIM] = o[r * WINDOW:(r + 1) * WINDOW].astype(o_ref.dtype)


def _rope_tables(n):
    t = np.arange(n)
    row = jnp.asarray(t // GRID_W, F32)
    col = jnp.asarray(t % GRID_W, F32)
    inv = ROPE_BASE ** (-jnp.arange(ROPE_FREQS, dtype=F32) / ROPE_FREQS)
    ar, ac = row[:, None] * inv, col[:, None] * inv
    cos_t = jnp.concatenate([jnp.cos(ar), jnp.cos(ar), jnp.cos(ac), jnp.cos(ac)], axis=1)
    sin_t = jnp.concatenate([-jnp.sin(ar), jnp.sin(ar), -jnp.sin(ac), jnp.sin(ac)], axis=1)
    return cos_t, sin_t


def _attn_win(qkv, kc, vc, sink, o_buf, n_prompt_rows, n_batch):
    wq = N_HEADS * HEAD_DIM
    wk = N_KV_HEADS * HEAD_DIM
    cos_t, sin_t = _rope_tables(DEC_SEQ)
    nqb = DEC_SEQ // WINDOW
    qb0 = n_prompt_rows // WINDOW
    sb0 = n_prompt_rows // DEC_SEQ
    past = kc.shape[1]
    return pl.pallas_call(
        _attn_win_kernel,
        grid=(n_batch, nqb),
        in_specs=[pl.BlockSpec(memory_space=pltpu.SMEM),
                  pl.BlockSpec((WINDOW, wq), lambda b, i: (qb0 + b * nqb + i, 0)),
                  pl.BlockSpec((DEC_SEQ, wk), lambda b, i: (sb0 + b, wq // wk)),
                  pl.BlockSpec((DEC_SEQ, wk), lambda b, i: (sb0 + b, wq // wk + 1)),
                  pl.BlockSpec((1, past, N_KV_HEADS, HEAD_DIM), lambda b, i: (b, 0, 0, 0)),
                  pl.BlockSpec((1, past, N_KV_HEADS, HEAD_DIM), lambda b, i: (b, 0, 0, 0)),
                  pl.BlockSpec((WINDOW, HEAD_DIM), lambda b, i: (i, 0)),
                  pl.BlockSpec((WINDOW, HEAD_DIM), lambda b, i: (i, 0)),
                  pl.BlockSpec((DEC_SEQ, HEAD_DIM), lambda b, i: (0, 0)),
                  pl.BlockSpec((DEC_SEQ, HEAD_DIM), lambda b, i: (0, 0)),
                  pl.BlockSpec(memory_space=pl.ANY)],
        out_specs=pl.BlockSpec((WINDOW, wq), lambda b, i: (qb0 + b * nqb + i, 0)),
        out_shape=jax.ShapeDtypeStruct(o_buf.shape, o_buf.dtype),
        scratch_shapes=[pltpu.VMEM((DEC_SEQ, wk), BF16), pltpu.VMEM((past, wk), BF16),
                        pltpu.VMEM((past, wk), BF16)],
        input_output_aliases={10: 0},
        compiler_params=_params("arbitrary", "arbitrary"),
        name="attn_win",
    )(sink, qkv, qkv, qkv, kc, vc, cos_t, sin_t, cos_t, sin_t, o_buf)


def _na_row_start(r):
    return min(max(r - NA_ROWS // 2, 0), GRID_ROWS - NA_ROWS)


def _attn_na_kernel(q_ref, k_ref, v_ref, kc_ref, vc_ref, bias_ref, _, o_ref):
    q = (q_ref[...] * ATTN_SCALE).astype(BF16)
    k = k_ref[...].astype(BF16)
    v = v_ref[...].astype(BF16)
    kc = kc_ref[0].astype(BF16)
    vc = vc_ref[0].astype(BF16)
    s_ctx = _dot_nt(q, kc)
    span = NA_ROWS * GRID_W
    rows = [slice(r * GRID_W, (r + 1) * GRID_W) for r in range(GRID_ROWS)]
    keys = [slice(_na_row_start(r) * GRID_W, _na_row_start(r) * GRID_W + span) for r in range(GRID_ROWS)]
    s_nbr = [_dot_nt(q[rows[r]], k[keys[r]]) + bias_ref[0, _na_row_start(r) - r + NA_ROWS - 1]
             for r in range(GRID_ROWS)]
    probs = []
    for r in range(GRID_ROWS):
        (p_c, p_n), inv = _softmax_parts([s_ctx[rows[r]], s_nbr[r]])
        probs.append((p_c.astype(BF16), p_n.astype(BF16), inv))
    for r, (p_c, p_n, inv) in enumerate(probs):
        o = (_dot(p_c, vc) + _dot(p_n, v[keys[r]])) * inv
        o_ref[rows[r], :] = o.astype(o_ref.dtype)


def _na_bias_table(rpb):
    cq = np.arange(GRID_W)[:, None]
    ck = np.arange(GRID_W)[None, :]
    cs = np.clip(cq - NA_COLS // 2, 0, GRID_W - NA_COLS)
    colmask = (ck >= cs) & (ck < cs + NA_COLS)
    dc = np.clip(ck - cq + NA_COLS - 1, 0, 2 * NA_COLS - 2)
    full = jnp.where(colmask, rpb[:, :, dc].astype(F32), NEG_INF)
    tabs = [jnp.transpose(full[:, off:off + NA_ROWS], (0, 2, 1, 3)).reshape(N_HEADS, GRID_W, NA_ROWS * GRID_W)
            for off in range(NA_ROWS)]
    return jnp.stack(tabs, axis=1)


def _attn_na(qkv, kc, vc, rpb, o_buf, n_prompt_rows, n_batch):
    sb0 = n_prompt_rows // DEC_SEQ
    kc = kc.reshape(n_batch, -1, N_HEADS * HEAD_DIM)
    vc = vc.reshape(n_batch, -1, N_HEADS * HEAD_DIM)
    past = kc.shape[1]
    bias = _na_bias_table(rpb)
    return pl.pallas_call(
        _attn_na_kernel,
        grid=(N_HEADS, n_batch),
        in_specs=[pl.BlockSpec((DEC_SEQ, HEAD_DIM), lambda h, b: (sb0 + b, h)),
                  pl.BlockSpec((DEC_SEQ, HEAD_DIM), lambda h, b: (sb0 + b, N_HEADS + h)),
                  pl.BlockSpec((DEC_SEQ, HEAD_DIM), lambda h, b: (sb0 + b, 2 * N_HEADS + h)),
                  pl.BlockSpec((1, past, HEAD_DIM), lambda h, b: (b, 0, h)),
                  pl.BlockSpec((1, past, HEAD_DIM), lambda h, b: (b, 0, h)),
                  pl.BlockSpec((1, NA_ROWS, GRID_W, NA_ROWS * GRID_W), lambda h, b: (h, 0, 0, 0)),
                  pl.BlockSpec(memory_space=pl.ANY)],
        out_specs=pl.BlockSpec((DEC_SEQ, HEAD_DIM), lambda h, b: (sb0 + b, h)),
        out_shape=jax.ShapeDtypeStruct(o_buf.shape, o_buf.dtype),
        input_output_aliases={6: 0},
        compiler_params=_params("arbitrary", "arbitrary"),
        name="attn_na",
    )(qkv, qkv, qkv, kc, vc, bias, o_buf)


SGU_CHUNKS = 2


def _sgu_kernel(z_ref, vg_ref, ws_ref, bs_ref, o_ref):
    sg = z_ref.shape[1] // 2
    for c in range(SGU_CHUNKS):
        rows = slice(c * CHUNK, (c + 1) * CHUNK)
        v = z_ref[rows, sg:].astype(F32)
        vn = v * lax.rsqrt(jnp.mean(v * v, axis=-1, keepdims=True) + NORM_EPS) * vg_ref[...]
        for g in range(SG_GROUPS):
            cols = slice(g * CHUNK, (g + 1) * CHUNK)
            f = _dot(ws_ref[g].astype(BF16), vn[:, cols].astype(BF16)) + bs_ref[:, g:g + 1]
            o_ref[rows, cols] = (z_ref[rows, cols].astype(F32) * f).astype(o_ref.dtype)


def _sgu(z, v_g, w_s, b_s):
    n, two_sg = z.shape
    sg = two_sg // 2
    tm = SGU_CHUNKS * CHUNK
    return pl.pallas_call(
        _sgu_kernel,
        grid=(n // tm,),
        in_specs=[pl.BlockSpec((tm, two_sg), lambda i: (i, 0)),
                  pl.BlockSpec((1, sg), lambda i: (0, 0)),
                  pl.BlockSpec((SG_GROUPS, CHUNK, CHUNK), lambda i: (0, 0, 0)),
                  pl.BlockSpec((CHUNK, SG_GROUPS), lambda i: (0, 0))],
        out_specs=pl.BlockSpec((tm, sg), lambda i: (i, 0)),
        out_shape=jax.ShapeDtypeStruct((n, sg), BF16),
        compiler_params=_params("arbitrary"),
        name="sgu",
    )(z, v_g.reshape(1, sg), w_s, b_s.T)


def _route_kernel(h_ref, wr_ref, *rest, seq, cap, aliased):
    if aliased:
        rest = rest[2:]
    xe_ref, gate_ref, pt_ref = rest
    h = h_ref[...]
    w = wr_ref[...]
    w_hi = w.astype(BF16)
    w_lo = (w - w_hi.astype(F32)).astype(BF16)
    both = _dot(h, jnp.concatenate([w_hi, w_lo], axis=1))
    logits = both[:, :LANES] + both[:, LANES:]
    lane = lax.broadcasted_iota(jnp.int32, (seq, LANES), 1)
    logits = jnp.where(lane < N_EXPERTS, logits, NEG_INF)
    ex = jnp.exp(logits - logits.max(axis=-1, keepdims=True))
    aff = ex / ex.sum(axis=-1, keepdims=True)

    def count(mask):
        ones = jnp.where(mask, 1.0, 0.0).reshape(COUNT_SPLIT, seq // COUNT_SPLIT, LANES)
        return jnp.sum(jnp.sum(ones, axis=0), axis=0, keepdims=True)

    def refine(i, ans):
        cand = ans | jnp.left_shift(jnp.int32(1), 30 - i)
        return jnp.where(count(aff >= pltpu.bitcast(cand, F32)) >= cap, cand, ans)

    thr = lax.fori_loop(0, 31, refine, jnp.zeros((1, LANES), jnp.int32))
    ge = aff >= pltpu.bitcast(thr, F32)
    gt = aff >= pltpu.bitcast(jnp.maximum(thr + 1, SMALLEST_NORMAL_BITS), F32)
    gt_f = jnp.where(gt, 1.0, 0.0)
    eq_f = jnp.where(ge, 1.0, 0.0) - gt_f
    need = cap - count(gt)
    tri = jnp.where(lax.broadcasted_iota(jnp.int32, (seq, seq), 1)
                    < lax.broadcasted_iota(jnp.int32, (seq, seq), 0), 1.0, 0.0).astype(BF16)
    before = _dot(tri, jnp.concatenate([gt_f, eq_f], axis=1).astype(BF16))
    gt_before, eq_before = before[:, :LANES], before[:, LANES:]
    sel = gt | (ge & (eq_before < need))
    slot = gt_before + jnp.minimum(eq_before, need)
    tgt = jnp.where(sel, slot, UNSELECTED)
    tgt_t = tgt.T
    aff_t = aff.T

    sub = lax.broadcasted_iota(jnp.int32, (cap, seq), 0).astype(F32)
    group = max(1, 256 // cap)
    for e0 in range(0, N_EXPERTS, group):
        hits = [sub == tgt_t[e:e + 1, :] for e in range(e0, e0 + group)]
        for k, hit in enumerate(hits):
            gate_ref[e0 + k] = jnp.sum(jnp.where(hit, aff_t[e0 + k:e0 + k + 1, :], 0.0), axis=1, keepdims=True)
        onehot = jnp.concatenate([jnp.where(hit, 1.0, 0.0) for hit in hits], axis=0).astype(BF16)
        rows = _dot(onehot, h).astype(BF16)
        for k in range(group):
            xe_ref[e0 + k] = rows[k * cap:(k + 1) * cap]

    lane_f = lane.astype(F32)
    for j in range(N_EXPERTS * cap // LANES):
        chunk = jnp.zeros((seq, LANES), F32)
        for e in range(j * LANES // cap, ((j + 1) * LANES - 1) // cap + 1):
            chunk = jnp.where(lane_f == tgt[:, e:e + 1] + float(e * cap - j * LANES), 1.0, chunk)
        pt_ref[0, :, j * LANES:(j + 1) * LANES] = chunk.astype(BF16)


def _route_gather(h, w_router_pad, seq, n_batch, row0, slot0, n_slots, prev=None):
    d = h.shape[1]
    cap = EC_FACTOR * seq // N_EXPERTS
    hb0 = row0 // seq
    sb0 = slot0 // cap
    in_specs = [pl.BlockSpec((seq, d), lambda b: (hb0 + b, 0)),
                pl.BlockSpec((d, LANES), lambda b: (0, 0))]
    args = [h, w_router_pad]
    aliases = {}
    if prev is not None:
        in_specs += [pl.BlockSpec(memory_space=pl.ANY), pl.BlockSpec(memory_space=pl.ANY)]
        args += list(prev)
        aliases = {2: 0, 3: 1}
    return pl.pallas_call(
        functools.partial(_route_kernel, seq=seq, cap=cap, aliased=prev is not None),
        grid=(n_batch,),
        in_specs=in_specs,
        out_specs=[pl.BlockSpec((N_EXPERTS, cap, d), lambda b: (0, sb0 + b, 0)),
                   pl.BlockSpec((N_EXPERTS, cap, 1), lambda b: (0, sb0 + b, 0)),
                   pl.BlockSpec((1, seq, N_EXPERTS * cap), lambda b: (b, 0, 0))],
        out_shape=[jax.ShapeDtypeStruct((N_EXPERTS, n_slots, d), BF16),
                   jax.ShapeDtypeStruct((N_EXPERTS, n_slots, 1), F32),
                   jax.ShapeDtypeStruct((n_batch, seq, N_EXPERTS * cap), BF16)],
        input_output_aliases=aliases,
        compiler_params=_params("arbitrary"),
        name="route_gather",
    )(*args)


FFN_ROWS = 256


def _ffn_up_kernel(x_ref, wg_ref, wu_ref, o_ref, wgb_ref, wub_ref):
    wgb_ref[...] = wg_ref[0].astype(BF16)
    wub_ref[...] = wu_ref[0].astype(BF16)
    for m in range(x_ref.shape[1] // FFN_ROWS):
        rows = slice(m * FFN_ROWS, (m + 1) * FFN_ROWS)
        x = x_ref[0, rows, :]
        o_ref[0, rows, :] = (_silu(_dot(x, wgb_ref[...])) * _dot(x, wub_ref[...])).astype(o_ref.dtype)


def _ffn_up(xe, w_gate, w_up, layer, tf=512):
    ne, r, d = xe.shape
    f = w_gate.shape[3]
    return pl.pallas_call(
        _ffn_up_kernel,
        grid=(ne, f // tf),
        in_specs=[pl.BlockSpec((1, r, d), lambda e, j: (e, 0, 0)),
                  pl.BlockSpec((None, 1, d, tf), lambda e, j: (layer, e, 0, j)),
                  pl.BlockSpec((None, 1, d, tf), lambda e, j: (layer, e, 0, j))],
        out_specs=pl.BlockSpec((1, r, tf), lambda e, j: (e, 0, j)),
        out_shape=jax.ShapeDtypeStruct((ne, r, f), BF16),
        scratch_shapes=[pltpu.VMEM((d, tf), BF16), pltpu.VMEM((d, tf), BF16)],
        compiler_params=_params("arbitrary", "arbitrary"),
        name="ffn_up",
    )(xe, w_gate, w_up)


def _ffn_down_kernel(h_ref, wd_ref, g_ref, o_ref, wdb_ref):
    wdb_ref[...] = wd_ref[0].astype(BF16)
    for m in range(h_ref.shape[1] // FFN_ROWS):
        rows = slice(m * FFN_ROWS, (m + 1) * FFN_ROWS)
        o_ref[0, rows, :] = (_dot(h_ref[0, rows, :], wdb_ref[...]) * g_ref[0, rows, :]).astype(o_ref.dtype)


def _ffn_down(hmid, w_down, gate, layer, tn=2048):
    ne, r, f = hmid.shape
    d = w_down.shape[3]
    return pl.pallas_call(
        _ffn_down_kernel,
        grid=(ne, d // tn),
        in_specs=[pl.BlockSpec((1, r, f), lambda e, j: (e, 0, 0)),
                  pl.BlockSpec((None, 1, f, tn), lambda e, j: (layer, e, 0, j)),
                  pl.BlockSpec((1, r, 1), lambda e, j: (e, 0, 0))],
        out_specs=pl.BlockSpec((1, r, tn), lambda e, j: (e, 0, j)),
        out_shape=jax.ShapeDtypeStruct((ne, r, d), BF16),
        scratch_shapes=[pltpu.VMEM((f, tn), BF16)],
        compiler_params=_params("arbitrary", "arbitrary"),
        name="ffn_down",
    )(hmid, w_down, gate)


COMBINE_ROWS = 512


def _moe_tokens(pt_ref, y_ref, x_ref, g_ref):
    ne, cap, d = y_ref.shape
    return x_ref[...] + g_ref[0] * _dot(pt_ref[0], y_ref[...].reshape(ne * cap, d))


def _combine_next_kernel(pt_ref, y_ref, x_ref, g_ref, ng_ref, sh_ref, sc_ref, *rest):
    o_ref, h_ref = rest[-2:]
    xn = _moe_tokens(pt_ref, y_ref, x_ref, g_ref)
    o_ref[...] = xn
    h_ref[...] = _norm_modulate(xn, ng_ref[0], sh_ref[0], sc_ref[0]).astype(h_ref.dtype)


def _combine_final_kernel(pt_ref, y_ref, x_ref, g_ref, ng_ref, o_ref):
    xn = _moe_tokens(pt_ref, y_ref, x_ref, g_ref)
    o_ref[...] = xn * lax.rsqrt(jnp.mean(xn * xn, axis=-1, keepdims=True) + NORM_EPS) * ng_ref[...]


def _combine(pt, y, x, mod, layer, seq, row0, slot0, n_prompt_rows, nxt, h_prev=None):
    n_batch = pt.shape[0]
    d = x.shape[1]
    cap = EC_FACTOR * seq // N_EXPERTS
    tr = min(seq, COMBINE_ROWS)
    per_seq = seq // tr
    xt0 = row0 // tr
    sb0 = slot0 // cap

    def tile(b, t):
        return xt0 + b * per_seq + t

    def mod_spec(lyr, which):
        return pl.BlockSpec(
            (1, 1, d), lambda b, t: ((lyr * COND_ROWS + _cond_of_tile(tile(b, t), tr, n_prompt_rows)) * 6 + which,
                                     0, 0))

    in_specs = [pl.BlockSpec((1, tr, N_EXPERTS * cap), lambda b, t: (b, t, 0)),
                pl.BlockSpec((N_EXPERTS, cap, d), lambda b, t: (0, sb0 + b, 0)),
                pl.BlockSpec((tr, d), lambda b, t: (tile(b, t), 0)),
                mod_spec(layer, 5)]
    if isinstance(nxt, tuple):
        g_next, l_next = nxt
        in_specs += [pl.BlockSpec((1, 1, d), lambda b, t: (l_next, 0, 0)), mod_spec(l_next, 0), mod_spec(l_next, 1)]
        args = [pt, y, x, mod, g_next, mod, mod]
        aliases = {2: 0}
        if h_prev is not None:
            in_specs.append(pl.BlockSpec(memory_space=pl.ANY))
            args.append(h_prev)
            aliases[7] = 1
        return pl.pallas_call(
            _combine_next_kernel,
            grid=(n_batch, per_seq),
            in_specs=in_specs,
            out_specs=[pl.BlockSpec((tr, d), lambda b, t: (tile(b, t), 0)),
                       pl.BlockSpec((tr, d), lambda b, t: (tile(b, t), 0))],
            out_shape=[jax.ShapeDtypeStruct(x.shape, x.dtype), jax.ShapeDtypeStruct(x.shape, BF16)],
            input_output_aliases=aliases,
            compiler_params=_params("arbitrary", "arbitrary"),
            name="moe_combine",
        )(*args)
    in_specs.append(pl.BlockSpec((1, d), lambda b, t: (0, 0)))
    return pl.pallas_call(
        _combine_final_kernel,
        grid=(n_batch, per_seq),
        in_specs=in_specs,
        out_specs=pl.BlockSpec((tr, d), lambda b, t: (b * per_seq + t, 0)),
        out_shape=jax.ShapeDtypeStruct((n_batch * seq, d), F32),
        compiler_params=_params("arbitrary", "arbitrary"),
        name="moe_combine_final",
    )(pt, y, x, mod, nxt)


def _moe(h, x, mod, layer, w_router, w_gate, w_up, w_down, n_prompt, n_latent, nxt):
    n_prompt_rows = n_prompt * SEQ
    cap_p = EC_FACTOR * SEQ // N_EXPERTS
    cap_s = EC_FACTOR * DEC_SEQ // N_EXPERTS
    slots_p = n_prompt * cap_p
    n_slots = slots_p + n_latent * cap_s
    wr = jnp.pad(w_router[layer], ((0, 0), (0, LANES - N_EXPERTS)))
    xe, gate, pt_p = _route_gather(h, wr, SEQ, n_prompt, 0, 0, n_slots)
    xe, gate, pt_s = _route_gather(h, wr, DEC_SEQ, n_latent, n_prompt_rows, slots_p, n_slots, prev=(xe, gate))
    y = _ffn_down(_ffn_up(xe, w_gate, w_up, layer), w_down, gate, layer)
    if isinstance(nxt, tuple):
        x, h = _combine(pt_p, y, x, mod, layer, SEQ, 0, 0, n_prompt_rows, nxt)
        return _combine(pt_s, y, x, mod, layer, DEC_SEQ, n_prompt_rows, slots_p, n_prompt_rows, nxt, h_prev=h)
    return (_combine(pt_p, y, x, mod, layer, SEQ, 0, 0, n_prompt_rows, nxt),
            _combine(pt_s, y, x, mod, layer, DEC_SEQ, n_prompt_rows, slots_p, n_prompt_rows, nxt))


def kernel(x_prompt, x_sample, cache_k_0, cache_v_0, cache_k_1, cache_v_1, cache_k_3, cache_v_3, c, c_ctx, norm1_g, norm2_g, w_mod, b_mod, a_w_qkv, a_sink, a_w_o, b_w_qkv, b_rpb, b_w_o, c_w_in, c_v_g, c_w_s, c_b_s, c_w_out, w_router, w_gate, w_up, w_down, final_g):
    n_prompt, _, d = x_prompt.shape
    n_latent = x_sample.shape[0]
    depth = w_mod.shape[0]
    p_rows = n_prompt * SEQ
    s_rows = n_latent * DEC_SEQ
    caches = {0: (cache_k_0, cache_v_0), 1: (cache_k_1, cache_v_1), 3: (cache_k_3, cache_v_3)}

    cond = jnp.concatenate([c_ctx[None], c, jnp.zeros((COND_ROWS - 1 - n_latent, d), F32)], axis=0)
    mod = _adaln_all(cond, w_mod, b_mod).reshape(depth * COND_ROWS * 6, 1, d)
    g1 = norm1_g.reshape(depth, 1, d)
    g2 = norm2_g.reshape(depth, 1, d)
    x = (x_prompt.reshape(p_rows, d), x_sample.reshape(s_rows, d))
    h = _enter_stream(x[0], 0, p_rows + s_rows, g1, mod, p_rows)
    h = _enter_stream(x[1], p_rows, p_rows + s_rows, g1, mod, p_rows, prev=h)

    new_state = []
    for l in range(depth):
        kind, j = l % 3, l // 3
        if kind == 0:
            qkv = _matmul(h, a_w_qkv, j, F32)
            o, k_new, v_new = _attn_ctx(qkv, a_sink[j], p_rows + s_rows, n_prompt, N_KV_HEADS)
            o = _attn_win(qkv, *caches[l], a_sink[j], o, p_rows, n_latent)
            w_out = a_w_o[j]
            new_state += [k_new, v_new]
        elif kind == 1:
            qkv = _matmul(h, b_w_qkv, j, F32)
            o, k_new, v_new = _attn_ctx(qkv, None, p_rows + s_rows, n_prompt, N_HEADS)
            o = _attn_na(qkv, *caches[l], b_rpb[j], o, p_rows, n_latent)
            w_out = b_w_o[j]
            new_state += [k_new, v_new]
        else:
            z = _matmul(h, c_w_in, j, BF16, act="gelu")
            o = _sgu(z, c_v_g[j], c_w_s[j], c_b_s[j])
            w_out = c_w_out[j]
        x, h = _matmul_residual(o, w_out.astype(BF16), x, mod, g2, l, p_rows)
        if l + 1 < depth:
            x, h = _moe(h, x, mod, l, w_router, w_gate, w_up, w_down, n_prompt, n_latent, (g1, l + 1))
    y_prompt, y_sample = _moe(h, x, mod, depth - 1, w_router, w_gate, w_up, w_down, n_prompt, n_latent,
                              final_g.reshape(1, d))
    return (y_prompt.reshape(n_prompt, SEQ, d), y_sample.reshape(n_latent, DEC_SEQ, d), *new_state)
```

```python
import functools

import numpy as np
import jax
import jax.numpy as jnp
from jax import lax
from jax.experimental import pallas as pl
from jax.experimental.pallas import tpu as pltpu

F32 = jnp.float32
BF16 = jnp.bfloat16

D_MODEL = 2048
HEAD_DIM = 128
N_HEADS = D_MODEL // HEAD_DIM
N_KV_HEADS = 4
KV_REP = N_HEADS // N_KV_HEADS
ATTN_SCALE = HEAD_DIM ** -0.5
SEQ = 256
DEC_SEQ = 1024
WINDOW = 128
WIN_SPAN = 3 * WINDOW
GRID_W = 64
GRID_ROWS = DEC_SEQ // GRID_W
NA_ROWS = 8
NA_COLS = 16
ROPE_BASE = 10000.0
ROPE_FREQS = HEAD_DIM // 4
CHUNK = 128
SG_GROUPS = 16
N_EXPERTS = 16
EC_FACTOR = 2
NORM_EPS = 1e-6
NEG_INF = -1e30
COND_ROWS = 16
LANES = 128
SOFTMAX_ROWS = 64
MM_ROWS = 256
COUNT_SPLIT = 8
SMALLEST_NORMAL_BITS = 0x00800000
UNSELECTED = -4096.0
VMEM_LIMIT = 56 * 1024 * 1024


def _params(*sem):
    return pltpu.CompilerParams(dimension_semantics=sem, vmem_limit_bytes=VMEM_LIMIT)


def _dot(a, b):
    return jnp.dot(a, b, preferred_element_type=F32)


def _dot_nt(a, b):
    return lax.dot_general(a, b, (((1,), (1,)), ((), ())), preferred_element_type=F32)


def _silu(x):
    return x / (1.0 + jnp.exp(-x))


def _cond_of_tile(i, tm, n_prompt_rows):
    n_p = n_prompt_rows // tm
    per_b = DEC_SEQ // tm
    return jnp.where(i < n_p, 0, 1 + (i - n_p) // per_b)


def _adaln_kernel(c_ref, w_ref, b_ref, o_ref):
    s = _silu(c_ref[...]).astype(BF16)
    o_ref[0] = _dot(s, w_ref[0].astype(BF16)) + b_ref[0]


def _adaln_all(cond, w_mod, b_mod):
    depth, d, n = w_mod.shape
    tn = 2048
    return pl.pallas_call(
        _adaln_kernel,
        grid=(depth, n // tn),
        in_specs=[pl.BlockSpec((COND_ROWS, d), lambda l, j: (0, 0)),
                  pl.BlockSpec((1, d, tn), lambda l, j: (l, 0, j)),
                  pl.BlockSpec((1, 1, tn), lambda l, j: (l, 0, j))],
        out_specs=pl.BlockSpec((1, COND_ROWS, tn), lambda l, j: (l, 0, j)),
        out_shape=jax.ShapeDtypeStruct((depth, COND_ROWS, n), F32),
        compiler_params=_params("arbitrary", "arbitrary"),
        name="adaln",
    )(cond, w_mod, b_mod.reshape(depth, 1, n))


def _mod_spec(layer, which, tm, n_prompt_rows, row_axis, tn=D_MODEL, col_axis=None):
    def index(*ids):
        cond = _cond_of_tile(ids[row_axis], tm, n_prompt_rows)
        col = 0 if col_axis is None else ids[col_axis]
        return ((layer * COND_ROWS + cond) * 6 + which, 0, col)
    return pl.BlockSpec((1, 1, tn), index)


def _norm_modulate(x, ng, shift, scale):
    y = x * lax.rsqrt(jnp.mean(x * x, axis=-1, keepdims=True) + NORM_EPS) * ng
    return y * (1.0 + scale) + shift


def _modulate_kernel(x_ref, g_ref, sh_ref, sc_ref, *rest):
    h_ref = rest[-1]
    h_ref[...] = _norm_modulate(x_ref[...], g_ref[0], sh_ref[0], sc_ref[0]).astype(h_ref.dtype)


def _enter_stream(x_part, row0, n_rows, g_all, mod, n_prompt_rows, prev=None):
    n, d = x_part.shape
    tm = 1024
    t0 = row0 // tm
    in_specs = [pl.BlockSpec((tm, d), lambda i: (i, 0)),
                pl.BlockSpec((1, 1, d), lambda i: (0, 0, 0)),
                pl.BlockSpec((1, 1, d), lambda i: ((_cond_of_tile(t0 + i, tm, n_prompt_rows)) * 6, 0, 0)),
                pl.BlockSpec((1, 1, d), lambda i: ((_cond_of_tile(t0 + i, tm, n_prompt_rows)) * 6 + 1, 0, 0))]
    args = [x_part, g_all, mod, mod]
    aliases = {}
    if prev is not None:
        in_specs.append(pl.BlockSpec(memory_space=pl.ANY))
        args.append(prev)
        aliases = {4: 0}
    return pl.pallas_call(
        _modulate_kernel,
        grid=(n // tm,),
        in_specs=in_specs,
        out_specs=pl.BlockSpec((tm, d), lambda i: (t0 + i, 0)),
        out_shape=jax.ShapeDtypeStruct((n_rows, d), BF16),
        input_output_aliases=aliases,
        compiler_params=_params("arbitrary"),
        name="enter_stream",
    )(*args)


def _mm_kernel(x_ref, w_ref, o_ref, wb_ref, *, act):
    @pl.when(pl.program_id(1) == 0)
    def _():
        wb_ref[...] = w_ref[...].astype(BF16)

    for c in range(x_ref.shape[0] // MM_ROWS):
        rows = slice(c * MM_ROWS, (c + 1) * MM_ROWS)
        acc = _dot(x_ref[rows, :], wb_ref[...])
        if act == "gelu":
            acc = jax.nn.gelu(acc)
        o_ref[rows, :] = acc.astype(o_ref.dtype)


def _matmul(x, w_all, widx, out_dtype, act=None, tm=1024, tn=1024):
    m, k = x.shape
    n = w_all.shape[2]
    return pl.pallas_call(
        functools.partial(_mm_kernel, act=act),
        grid=(n // tn, m // tm),
        in_specs=[pl.BlockSpec((tm, k), lambda j, i: (i, 0)),
                  pl.BlockSpec((None, k, tn), lambda j, i: (widx, 0, j))],
        out_specs=pl.BlockSpec((tm, tn), lambda j, i: (i, j)),
        out_shape=jax.ShapeDtypeStruct((m, n), out_dtype),
        scratch_shapes=[pltpu.VMEM((k, tn), BF16)],
        compiler_params=_params("arbitrary", "arbitrary"),
        name="matmul",
    )(x, w_all)


def _mm_res_kernel(x_ref, w_ref, *rest, n_prompt_tiles):
    if n_prompt_tiles is None:
        r_ref = rest[0]
        rest = rest[1:]
    else:
        rp_ref, rs_ref = rest[:2]
        rest = rest[2:]
        from_prompt = pl.program_id(0) < n_prompt_tiles
    g_ref, ng_ref, sh_ref, sc_ref, o_ref, h_ref = rest
    for c in range(x_ref.shape[0] // MM_ROWS):
        rows = slice(c * MM_ROWS, (c + 1) * MM_ROWS)
        if n_prompt_tiles is None:
            r = r_ref[rows, :]
        else:
            r = jnp.where(from_prompt, rp_ref[rows, :], rs_ref[rows, :])
        xn = r + g_ref[0] * _dot(x_ref[rows, :], w_ref[...])
        o_ref[rows, :] = xn
        h_ref[rows, :] = _norm_modulate(xn, ng_ref[0], sh_ref[0], sc_ref[0]).astype(h_ref.dtype)


def _matmul_residual(x, w, res, mod, g2, layer, n_prompt_rows, tm=512):
    m, k = x.shape
    n = w.shape[1]
    if isinstance(res, tuple):
        pt = n_prompt_rows // tm
        res_specs = [pl.BlockSpec((tm, n), lambda i: (jnp.minimum(i, pt - 1), 0)),
                     pl.BlockSpec((tm, n), lambda i: (jnp.maximum(i - pt, 0), 0))]
        res_args, aliases = list(res), {}
    else:
        pt = None
        res_specs = [pl.BlockSpec((tm, n), lambda i: (i, 0))]
        res_args, aliases = [res], {2: 0}
    return pl.pallas_call(
        functools.partial(_mm_res_kernel, n_prompt_tiles=pt),
        grid=(m // tm,),
        in_specs=[pl.BlockSpec((tm, k), lambda i: (i, 0)),
                  pl.BlockSpec((k, n), lambda i: (0, 0)),
                  *res_specs,
                  _mod_spec(layer, 2, tm, n_prompt_rows, 0),
                  pl.BlockSpec((1, 1, n), lambda i: (layer, 0, 0)),
                  _mod_spec(layer, 3, tm, n_prompt_rows, 0),
                  _mod_spec(layer, 4, tm, n_prompt_rows, 0)],
        out_specs=[pl.BlockSpec((tm, n), lambda i: (i, 0)),
                   pl.BlockSpec((tm, n), lambda i: (i, 0))],
        out_shape=[jax.ShapeDtypeStruct((m, n), F32), jax.ShapeDtypeStruct((m, n), BF16)],
        input_output_aliases=aliases,
        compiler_params=_params("arbitrary"),
        name="matmul_residual",
    )(x, w, *res_args, mod, g2, mod, mod)


def _softmax_parts(parts, sink=None):
    m = parts[0].max(axis=-1, keepdims=True)
    for s in parts[1:]:
        m = jnp.maximum(m, s.max(axis=-1, keepdims=True))
    if sink is not None:
        m = jnp.maximum(m, sink)
    ps = [jnp.exp(s - m) for s in parts]
    den = ps[0].sum(axis=-1, keepdims=True)
    for p in ps[1:]:
        den = den + p.sum(axis=-1, keepdims=True)
    if sink is not None:
        den = den + jnp.exp(sink - m)
    return ps, 1.0 / den


def _attn_ctx_kernel(sink_ref, q_ref, k_ref, v_ref, o_ref, ks_ref, vs_ref, *, rep, use_sink):
    k_b = v_b = None
    for h in range(N_HEADS):
        g = h // rep
        if h % rep == 0:
            k_f = k_ref[:, g * HEAD_DIM:(g + 1) * HEAD_DIM]
            v_f = v_ref[:, g * HEAD_DIM:(g + 1) * HEAD_DIM]
            ks_ref[0, :, g, :] = k_f
            vs_ref[0, :, g, :] = v_f
            k_b = k_f.astype(BF16)
            v_b = v_f.astype(BF16)
        q = (q_ref[:, h * HEAD_DIM:(h + 1) * HEAD_DIM] * ATTN_SCALE).astype(BF16)
        s = _dot_nt(q, k_b)
        (p,), inv = _softmax_parts([s], sink_ref[h] if use_sink else None)
        o = _dot(p.astype(BF16), v_b) * inv
        o_ref[:, h * HEAD_DIM:(h + 1) * HEAD_DIM] = o.astype(o_ref.dtype)


def _attn_ctx(qkv, sink, n_rows, n_batch, n_kv):
    wq = N_HEADS * HEAD_DIM
    wk = n_kv * HEAD_DIM
    use_sink = sink is not None
    if sink is None:
        sink = jnp.zeros((N_HEADS,), F32)
    return pl.pallas_call(
        functools.partial(_attn_ctx_kernel, rep=N_HEADS // n_kv, use_sink=use_sink),
        grid=(n_batch,),
        in_specs=[pl.BlockSpec(memory_space=pltpu.SMEM),
                  pl.BlockSpec((SEQ, wq), lambda b: (b, 0)),
                  pl.BlockSpec((SEQ, wk), lambda b: (b, wq // wk)),
                  pl.BlockSpec((SEQ, wk), lambda b: (b, wq // wk + 1))],
        out_specs=[pl.BlockSpec((SEQ, wq), lambda b: (b, 0)),
                   pl.BlockSpec((1, SEQ, n_kv, HEAD_DIM), lambda b: (b, 0, 0, 0)),
                   pl.BlockSpec((1, SEQ, n_kv, HEAD_DIM), lambda b: (b, 0, 0, 0))],
        out_shape=[jax.ShapeDtypeStruct((n_rows, wq), BF16),
                   jax.ShapeDtypeStruct((n_batch, SEQ, n_kv, HEAD_DIM), F32),
                   jax.ShapeDtypeStruct((n_batch, SEQ, n_kv, HEAD_DIM), F32)],
        compiler_params=_params("arbitrary"),
        name="attn_ctx",
    )(sink, qkv, qkv, qkv)


def _rope(x, cos_t, sin_t):
    lane = lax.broadcasted_iota(jnp.int32, x.shape, 1)
    partner = jnp.where((lane & ROPE_FREQS) == 0,
                        pltpu.roll(x, HEAD_DIM - ROPE_FREQS, 1), pltpu.roll(x, ROPE_FREQS, 1))
    return x * cos_t + partner * sin_t


def _attn_win_kernel(sink_ref, q_ref, k_ref, v_ref, kc_ref, vc_ref, cq_ref, sq_ref, ck_ref, sk_ref,
                     _, o_ref, kr_ref, kcb_ref, vcb_ref):
    i = pl.program_id(1)

    @pl.when(i == 0)
    def _():
        for g in range(N_KV_HEADS):
            cols = slice(g * HEAD_DIM, (g + 1) * HEAD_DIM)
            kr_ref[:, cols] = _rope(k_ref[:, cols], ck_ref[...], sk_ref[...]).astype(BF16)
            kcb_ref[:, cols] = kc_ref[0, :, g, :].astype(BF16)
            vcb_ref[:, cols] = vc_ref[0, :, g, :].astype(BF16)

    start = pl.multiple_of(jnp.clip(i * WINDOW - WINDOW, 0, DEC_SEQ - WIN_SPAN), WINDOW)
    qpos = i * WINDOW + lax.broadcasted_iota(jnp.int32, (WINDOW, WIN_SPAN), 0)
    kpos = start + lax.broadcasted_iota(jnp.int32, (WINDOW, WIN_SPAN), 1)
    band = jnp.where(jnp.abs(qpos - kpos) <= WINDOW, 0.0, NEG_INF)
    cq, sq = cq_ref[...], sq_ref[...]
    for g in range(N_KV_HEADS):
        cols = slice(g * HEAD_DIM, (g + 1) * HEAD_DIM)
        kw = kr_ref[pl.ds(start, WIN_SPAN), cols]
        vw = v_ref[pl.ds(start, WIN_SPAN), cols].astype(BF16)
        kc = kcb_ref[:, cols]
        vc = vcb_ref[:, cols]
        heads = range(g * KV_REP, (g + 1) * KV_REP)
        qs = jnp.concatenate(
            [_rope(q_ref[:, h * HEAD_DIM:(h + 1) * HEAD_DIM], cq, sq) * ATTN_SCALE for h in heads],
            axis=0).astype(BF16)
        s_w = _dot_nt(qs, kw)
        s_c = _dot_nt(qs, kc)
        pcs, pws, invs = [], [], []
        for t in range(KV_REP * WINDOW // SOFTMAX_ROWS):
            rows = slice(t * SOFTMAX_ROWS, (t + 1) * SOFTMAX_ROWS)
            q0 = (t * SOFTMAX_ROWS) % WINDOW
            sink = sink_ref[g * KV_REP + t * SOFTMAX_ROWS // WINDOW]
            (p_c, p_w), inv = _softmax_parts([s_c[rows], s_w[rows] + band[q0:q0 + SOFTMAX_ROWS]], sink)
            pcs.append(p_c.astype(BF16))
            pws.append(p_w.astype(BF16))
            invs.append(inv)
        o = (_dot(jnp.concatenate(pcs, axis=0), vc) + _dot(jnp.concatenate(pws, axis=0), vw)
             ) * jnp.concatenate(invs, axis=0)
        for r, h in enumerate(heads):
            o_ref[:, h * HEAD_DIM:(h + 1) * HEAD_DIM] = o[r * WINDOW:(r + 1) * WINDOW].astype(o_ref.dtype)


def _rope_tables(n):
    t = np.arange(n)
    row = jnp.asarray(t // GRID_W, F32)
    col = jnp.asarray(t % GRID_W, F32)
    inv = ROPE_BASE ** (-jnp.arange(ROPE_FREQS, dtype=F32) / ROPE_FREQS)
    ar, ac = row[:, None] * inv, col[:, None] * inv
    cos_t = jnp.concatenate([jnp.cos(ar), jnp.cos(ar), jnp.cos(ac), jnp.cos(ac)], axis=1)
    sin_t = jnp.concatenate([-jnp.sin(ar), jnp.sin(ar), -jnp.sin(ac), jnp.sin(ac)], axis=1)
    return cos_t, sin_t


def _attn_win(qkv, kc, vc, sink, o_buf, n_prompt_rows, n_batch):
    wq = N_HEADS * HEAD_DIM
    wk = N_KV_HEADS * HEAD_DIM
    cos_t, sin_t = _rope_tables(DEC_SEQ)
    nqb = DEC_SEQ // WINDOW
    qb0 = n_prompt_rows // WINDOW
    sb0 = n_prompt_rows // DEC_SEQ
    past = kc.shape[1]
    return pl.pallas_call(
        _attn_win_kernel,
        grid=(n_batch, nqb),
        in_specs=[pl.BlockSpec(memory_space=pltpu.SMEM),
                  pl.BlockSpec((WINDOW, wq), lambda b, i: (qb0 + b * nqb + i, 0)),
                  pl.BlockSpec((DEC_SEQ, wk), lambda b, i: (sb0 + b, wq // wk)),
                  pl.BlockSpec((DEC_SEQ, wk), lambda b, i: (sb0 + b, wq // wk + 1)),
                  pl.BlockSpec((1, past, N_KV_HEADS, HEAD_DIM), lambda b, i: (b, 0, 0, 0)),
                  pl.BlockSpec((1, past, N_KV_HEADS, HEAD_DIM), lambda b, i: (b, 0, 0, 0)),
                  pl.BlockSpec((WINDOW, HEAD_DIM), lambda b, i: (i, 0)),
                  pl.BlockSpec((WINDOW, HEAD_DIM), lambda b, i: (i, 0)),
                  pl.BlockSpec((DEC_SEQ, HEAD_DIM), lambda b, i: (0, 0)),
                  pl.BlockSpec((DEC_SEQ, HEAD_DIM), lambda b, i: (0, 0)),
                  pl.BlockSpec(memory_space=pl.ANY)],
        out_specs=pl.BlockSpec((WINDOW, wq), lambda b, i: (qb0 + b * nqb + i, 0)),
        out_shape=jax.ShapeDtypeStruct(o_buf.shape, o_buf.dtype),
        scratch_shapes=[pltpu.VMEM((DEC_SEQ, wk), BF16), pltpu.VMEM((past, wk), BF16),
                        pltpu.VMEM((past, wk), BF16)],
        input_output_aliases={10: 0},
        compiler_params=_params("arbitrary", "arbitrary"),
        name="attn_win",
    )(sink, qkv, qkv, qkv, kc, vc, cos_t, sin_t, cos_t, sin_t, o_buf)


def _na_row_start(r):
    return min(max(r - NA_ROWS // 2, 0), GRID_ROWS - NA_ROWS)


def _attn_na_kernel(q_ref, k_ref, v_ref, kc_ref, vc_ref, bias_ref, _, o_ref):
    q = (q_ref[...] * ATTN_SCALE).astype(BF16)
    k = k_ref[...].astype(BF16)
    v = v_ref[...].astype(BF16)
    kc = kc_ref[0].astype(BF16)
    vc = vc_ref[0].astype(BF16)
    s_ctx = _dot_nt(q, kc)
    span = NA_ROWS * GRID_W
    rows = [slice(r * GRID_W, (r + 1) * GRID_W) for r in range(GRID_ROWS)]
    keys = [slice(_na_row_start(r) * GRID_W, _na_row_start(r) * GRID_W + span) for r in range(GRID_ROWS)]
    s_nbr = [_dot_nt(q[rows[r]], k[keys[r]]) + bias_ref[0, _na_row_start(r) - r + NA_ROWS - 1]
             for r in range(GRID_ROWS)]
    probs = []
    for r in range(GRID_ROWS):
        (p_c, p_n), inv = _softmax_parts([s_ctx[rows[r]], s_nbr[r]])
        probs.append((p_c.astype(BF16), p_n.astype(BF16), inv))
    for r, (p_c, p_n, inv) in enumerate(probs):
        o = (_dot(p_c, vc) + _dot(p_n, v[keys[r]])) * inv
        o_ref[rows[r], :] = o.astype(o_ref.dtype)


def _na_bias_table(rpb):
    cq = np.arange(GRID_W)[:, None]
    ck = np.arange(GRID_W)[None, :]
    cs = np.clip(cq - NA_COLS // 2, 0, GRID_W - NA_COLS)
    colmask = (ck >= cs) & (ck < cs + NA_COLS)
    dc = np.clip(ck - cq + NA_COLS - 1, 0, 2 * NA_COLS - 2)
    full = jnp.where(colmask, rpb[:, :, dc].astype(F32), NEG_INF)
    tabs = [jnp.transpose(full[:, off:off + NA_ROWS], (0, 2, 1, 3)).reshape(N_HEADS, GRID_W, NA_ROWS * GRID_W)
            for off in range(NA_ROWS)]
    return jnp.stack(tabs, axis=1)


def _attn_na(qkv, kc, vc, rpb, o_buf, n_prompt_rows, n_batch):
    sb0 = n_prompt_rows // DEC_SEQ
    kc = kc.reshape(n_batch, -1, N_HEADS * HEAD_DIM)
    vc = vc.reshape(n_batch, -1, N_HEADS * HEAD_DIM)
    past = kc.shape[1]
    bias = _na_bias_table(rpb)
    return pl.pallas_call(
        _attn_na_kernel,
        grid=(N_HEADS, n_batch),
        in_specs=[pl.BlockSpec((DEC_SEQ, HEAD_DIM), lambda h, b: (sb0 + b, h)),
                  pl.BlockSpec((DEC_SEQ, HEAD_DIM), lambda h, b: (sb0 + b, N_HEADS + h)),
                  pl.BlockSpec((DEC_SEQ, HEAD_DIM), lambda h, b: (sb0 + b, 2 * N_HEADS + h)),
                  pl.BlockSpec((1, past, HEAD_DIM), lambda h, b: (b, 0, h)),
                  pl.BlockSpec((1, past, HEAD_DIM), lambda h, b: (b, 0, h)),
                  pl.BlockSpec((1, NA_ROWS, GRID_W, NA_ROWS * GRID_W), lambda h, b: (h, 0, 0, 0)),
                  pl.BlockSpec(memory_space=pl.ANY)],
        out_specs=pl.BlockSpec((DEC_SEQ, HEAD_DIM), lambda h, b: (sb0 + b, h)),
        out_shape=jax.ShapeDtypeStruct(o_buf.shape, o_buf.dtype),
        input_output_aliases={6: 0},
        compiler_params=_params("arbitrary", "arbitrary"),
        name="attn_na",
    )(qkv, qkv, qkv, kc, vc, bias, o_buf)


SGU_CHUNKS = 2


def _sgu_kernel(z_ref, vg_ref, ws_ref, bs_ref, o_ref):
    sg = z_ref.shape[1] // 2
    for c in range(SGU_CHUNKS):
        rows = slice(c * CHUNK, (c + 1) * CHUNK)
        v = z_ref[rows, sg:].astype(F32)
        vn = v * lax.rsqrt(jnp.mean(v * v, axis=-1, keepdims=True) + NORM_EPS) * vg_ref[...]
        for g in range(SG_GROUPS):
            cols = slice(g * CHUNK, (g + 1) * CHUNK)
            f = _dot(ws_ref[g].astype(BF16), vn[:, cols].astype(BF16)) + bs_ref[:, g:g + 1]
            o_ref[rows, cols] = (z_ref[rows, cols].astype(F32) * f).astype(o_ref.dtype)


def _sgu(z, v_g, w_s, b_s):
    n, two_sg = z.shape
    sg = two_sg // 2
    tm = SGU_CHUNKS * CHUNK
    return pl.pallas_call(
        _sgu_kernel,
        grid=(n // tm,),
        in_specs=[pl.BlockSpec((tm, two_sg), lambda i: (i, 0)),
                  pl.BlockSpec((1, sg), lambda i: (0, 0)),
                  pl.BlockSpec((SG_GROUPS, CHUNK, CHUNK), lambda i: (0, 0, 0)),
                  pl.BlockSpec((CHUNK, SG_GROUPS), lambda i: (0, 0))],
        out_specs=pl.BlockSpec((tm, sg), lambda i: (i, 0)),
        out_shape=jax.ShapeDtypeStruct((n, sg), BF16),
        compiler_params=_params("arbitrary"),
        name="sgu",
    )(z, v_g.reshape(1, sg), w_s, b_s.T)


def _route_kernel(h_ref, wr_ref, *rest, seq, cap, aliased):
    if aliased:
        rest = rest[2:]
    xe_ref, gate_ref, pt_ref = rest
    h = h_ref[...]
    w = wr_ref[...]
    w_hi = w.astype(BF16)
    w_lo = (w - w_hi.astype(F32)).astype(BF16)
    both = _dot(h, jnp.concatenate([w_hi, w_lo], axis=1))
    logits = both[:, :LANES] + both[:, LANES:]
    lane = lax.broadcasted_iota(jnp.int32, (seq, LANES), 1)
    logits = jnp.where(lane < N_EXPERTS, logits, NEG_INF)
    ex = jnp.exp(logits - logits.max(axis=-1, keepdims=True))
    aff = ex / ex.sum(axis=-1, keepdims=True)

    def count(mask):
        ones = jnp.where(mask, 1.0, 0.0).reshape(COUNT_SPLIT, seq // COUNT_SPLIT, LANES)
        return jnp.sum(jnp.sum(ones, axis=0), axis=0, keepdims=True)

    def refine(i, ans):
        cand = ans | jnp.left_shift(jnp.int32(1), 30 - i)
        return jnp.where(count(aff >= pltpu.bitcast(cand, F32)) >= cap, cand, ans)

    thr = lax.fori_loop(0, 31, refine, jnp.zeros((1, LANES), jnp.int32))
    ge = aff >= pltpu.bitcast(thr, F32)
    gt = aff >= pltpu.bitcast(jnp.maximum(thr + 1, SMALLEST_NORMAL_BITS), F32)
    gt_f = jnp.where(gt, 1.0, 0.0)
    eq_f = jnp.where(ge, 1.0, 0.0) - gt_f
    need = cap - count(gt)
    tri = jnp.where(lax.broadcasted_iota(jnp.int32, (seq, seq), 1)
                    < lax.broadcasted_iota(jnp.int32, (seq, seq), 0), 1.0, 0.0).astype(BF16)
    before = _dot(tri, jnp.concatenate([gt_f, eq_f], axis=1).astype(BF16))
    gt_before, eq_before = before[:, :LANES], before[:, LANES:]
    sel = gt | (ge & (eq_before < need))
    slot = gt_before + jnp.minimum(eq_before, need)
    tgt = jnp.where(sel, slot, UNSELECTED)
    tgt_t = tgt.T
    aff_t = aff.T

    sub = lax.broadcasted_iota(jnp.int32, (cap, seq), 0).astype(F32)
    group = max(1, 256 // cap)
    for e0 in range(0, N_EXPERTS, group):
        hits = [sub == tgt_t[e:e + 1, :] for e in range(e0, e0 + group)]
        for k, hit in enumerate(hits):
            gate_ref[e0 + k] = jnp.sum(jnp.where(hit, aff_t[e0 + k:e0 + k + 1, :], 0.0), axis=1, keepdims=True)
        onehot = jnp.concatenate([jnp.where(hit, 1.0, 0.0) for hit in hits], axis=0).astype(BF16)
        rows = _dot(onehot, h).astype(BF16)
        for k in range(group):
            xe_ref[e0 + k] = rows[k * cap:(k + 1) * cap]

    lane_f = lane.astype(F32)
    for j in range(N_EXPERTS * cap // LANES):
        chunk = jnp.zeros((seq, LANES), F32)
        for e in range(j * LANES // cap, ((j + 1) * LANES - 1) // cap + 1):
            chunk = jnp.where(lane_f == tgt[:, e:e + 1] + float(e * cap - j * LANES), 1.0, chunk)
        pt_ref[0, :, j * LANES:(j + 1) * LANES] = chunk.astype(BF16)


def _route_gather(h, w_router_pad, seq, n_batch, row0, slot0, n_slots, prev=None):
    d = h.shape[1]
    cap = EC_FACTOR * seq // N_EXPERTS
    hb0 = row0 // seq
    sb0 = slot0 // cap
    in_specs = [pl.BlockSpec((seq, d), lambda b: (hb0 + b, 0)),
                pl.BlockSpec((d, LANES), lambda b: (0, 0))]
    args = [h, w_router_pad]
    aliases = {}
    if prev is not None:
        in_specs += [pl.BlockSpec(memory_space=pl.ANY), pl.BlockSpec(memory_space=pl.ANY)]
        args += list(prev)
        aliases = {2: 0, 3: 1}
    return pl.pallas_call(
        functools.partial(_route_kernel, seq=seq, cap=cap, aliased=prev is not None),
        grid=(n_batch,),
        in_specs=in_specs,
        out_specs=[pl.BlockSpec((N_EXPERTS, cap, d), lambda b: (0, sb0 + b, 0)),
                   pl.BlockSpec((N_EXPERTS, cap, 1), lambda b: (0, sb0 + b, 0)),
                   pl.BlockSpec((1, seq, N_EXPERTS * cap), lambda b: (b, 0, 0))],
        out_shape=[jax.ShapeDtypeStruct((N_EXPERTS, n_slots, d), BF16),
                   jax.ShapeDtypeStruct((N_EXPERTS, n_slots, 1), F32),
                   jax.ShapeDtypeStruct((n_batch, seq, N_EXPERTS * cap), BF16)],
        input_output_aliases=aliases,
        compiler_params=_params("arbitrary"),
        name="route_gather",
    )(*args)


FFN_ROWS = 256


def _ffn_up_kernel(x_ref, wg_ref, wu_ref, o_ref, wgb_ref, wub_ref):
    wgb_ref[...] = wg_ref[0].astype(BF16)
    wub_ref[...] = wu_ref[0].astype(BF16)
    for m in range(x_ref.shape[1] // FFN_ROWS):
        rows = slice(m * FFN_ROWS, (m + 1) * FFN_ROWS)
        x = x_ref[0, rows, :]
        o_ref[0, rows, :] = (_silu(_dot(x, wgb_ref[...])) * _dot(x, wub_ref[...])).astype(o_ref.dtype)


def _ffn_up(xe, w_gate, w_up, layer, tf=512):
    ne, r, d = xe.shape
    f = w_gate.shape[3]
    return pl.pallas_call(
        _ffn_up_kernel,
        grid=(ne, f // tf),
        in_specs=[pl.BlockSpec((1, r, d), lambda e, j: (e, 0, 0)),
                  pl.BlockSpec((None, 1, d, tf), lambda e, j: (layer, e, 0, j)),
                  pl.BlockSpec((None, 1, d, tf), lambda e, j: (layer, e, 0, j))],
        out_specs=pl.BlockSpec((1, r, tf), lambda e, j: (e, 0, j)),
        out_shape=jax.ShapeDtypeStruct((ne, r, f), BF16),
        scratch_shapes=[pltpu.VMEM((d, tf), BF16), pltpu.VMEM((d, tf), BF16)],
        compiler_params=_params("arbitrary", "arbitrary"),
        name="ffn_up",
    )(xe, w_gate, w_up)


def _ffn_down_kernel(h_ref, wd_ref, g_ref, o_ref, wdb_ref):
    wdb_ref[...] = wd_ref[0].astype(BF16)
    for m in range(h_ref.shape[1] // FFN_ROWS):
        rows = slice(m * FFN_ROWS, (m + 1) * FFN_ROWS)
        o_ref[0, rows, :] = (_dot(h_ref[0, rows, :], wdb_ref[...]) * g_ref[0, rows, :]).astype(o_ref.dtype)


def _ffn_down(hmid, w_down, gate, layer, tn=2048):
    ne, r, f = hmid.shape
    d = w_down.shape[3]
    return pl.pallas_call(
        _ffn_down_kernel,
        grid=(ne, d // tn),
        in_specs=[pl.BlockSpec((1, r, f), lambda e, j: (e, 0, 0)),
                  pl.BlockSpec((None, 1, f, tn), lambda e, j: (layer, e, 0, j)),
                  pl.BlockSpec((1, r, 1), lambda e, j: (e, 0, 0))],
        out_specs=pl.BlockSpec((1, r, tn), lambda e, j: (e, 0, j)),
        out_shape=jax.ShapeDtypeStruct((ne, r, d), BF16),
        scratch_shapes=[pltpu.VMEM((f, tn), BF16)],
        compiler_params=_params("arbitrary", "arbitrary"),
        name="ffn_down",
    )(hmid, w_down, gate)


COMBINE_ROWS = 512


def _moe_tokens(pt_ref, y_ref, x_ref, g_ref):
    ne, cap, d = y_ref.shape
    return x_ref[...] + g_ref[0] * _dot(pt_ref[0], y_ref[...].reshape(ne * cap, d))


def _combine_next_kernel(pt_ref, y_ref, x_ref, g_ref, ng_ref, sh_ref, sc_ref, *rest):
    o_ref, h_ref = rest[-2:]
    xn = _moe_tokens(pt_ref, y_ref, x_ref, g_ref)
    o_ref[...] = xn
    h_ref[...] = _norm_modulate(xn, ng_ref[0], sh_ref[0], sc_ref[0]).astype(h_ref.dtype)


def _combine_final_kernel(pt_ref, y_ref, x_ref, g_ref, ng_ref, o_ref):
    xn = _moe_tokens(pt_ref, y_ref, x_ref, g_ref)
    o_ref[...] = xn * lax.rsqrt(jnp.mean(xn * xn, axis=-1, keepdims=True) + NORM_EPS) * ng_ref[...]


def _combine(pt, y, x, mod, layer, seq, row0, slot0, n_prompt_rows, nxt, h_prev=None):
    n_batch = pt.shape[0]
    d = x.shape[1]
    cap = EC_FACTOR * seq // N_EXPERTS
    tr = min(seq, COMBINE_ROWS)
    per_seq = seq // tr
    xt0 = row0 // tr
    sb0 = slot0 // cap

    def tile(b, t):
        return xt0 + b * per_seq + t

    def mod_spec(lyr, which):
        return pl.BlockSpec(
            (1, 1, d), lambda b, t: ((lyr * COND_ROWS + _cond_of_tile(tile(b, t), tr, n_prompt_rows)) * 6 + which,
                                     0, 0))

    in_specs = [pl.BlockSpec((1, tr, N_EXPERTS * cap), lambda b, t: (b, t, 0)),
                pl.BlockSpec((N_EXPERTS, cap, d), lambda b, t: (0, sb0 + b, 0)),
                pl.BlockSpec((tr, d), lambda b, t: (tile(b, t), 0)),
                mod_spec(layer, 5)]
    if isinstance(nxt, tuple):
        g_next, l_next = nxt
        in_specs += [pl.BlockSpec((1, 1, d), lambda b, t: (l_next, 0, 0)), mod_spec(l_next, 0), mod_spec(l_next, 1)]
        args = [pt, y, x, mod, g_next, mod, mod]
        aliases = {2: 0}
        if h_prev is not None:
            in_specs.append(pl.BlockSpec(memory_space=pl.ANY))
            args.append(h_prev)
            aliases[7] = 1
        return pl.pallas_call(
            _combine_next_kernel,
            grid=(n_batch, per_seq),
            in_specs=in_specs,
            out_specs=[pl.BlockSpec((tr, d), lambda b, t: (tile(b, t), 0)),
                       pl.BlockSpec((tr, d), lambda b, t: (tile(b, t), 0))],
            out_shape=[jax.ShapeDtypeStruct(x.shape, x.dtype), jax.ShapeDtypeStruct(x.shape, BF16)],
            input_output_aliases=aliases,
            compiler_params=_params("arbitrary", "arbitrary"),
            name="moe_combine",
        )(*args)
    in_specs.append(pl.BlockSpec((1, d), lambda b, t: (0, 0)))
    return pl.pallas_call(
        _combine_final_kernel,
        grid=(n_batch, per_seq),
        in_specs=in_specs,
        out_specs=pl.BlockSpec((tr, d), lambda b, t: (b * per_seq + t, 0)),
        out_shape=jax.ShapeDtypeStruct((n_batch * seq, d), F32),
        compiler_params=_params("arbitrary", "arbitrary"),
        name="moe_combine_final",
    )(pt, y, x, mod, nxt)


def _moe(h, x, mod, layer, w_router, w_gate, w_up, w_down, n_prompt, n_latent, nxt):
    n_prompt_rows = n_prompt * SEQ
    cap_p = EC_FACTOR * SEQ // N_EXPERTS
    cap_s = EC_FACTOR * DEC_SEQ // N_EXPERTS
    slots_p = n_prompt * cap_p
    n_slots = slots_p + n_latent * cap_s
    wr = jnp.pad(w_router[layer], ((0, 0), (0, LANES - N_EXPERTS)))
    xe, gate, pt_p = _route_gather(h, wr, SEQ, n_prompt, 0, 0, n_slots)
    xe, gate, pt_s = _route_gather(h, wr, DEC_SEQ, n_latent, n_prompt_rows, slots_p, n_slots, prev=(xe, gate))
    y = _ffn_down(_ffn_up(xe, w_gate, w_up, layer), w_down, gate, layer)
    if isinstance(nxt, tuple):
        x, h = _combine(pt_p, y, x, mod, layer, SEQ, 0, 0, n_prompt_rows, nxt)
        return _combine(pt_s, y, x, mod, layer, DEC_SEQ, n_prompt_rows, slots_p, n_prompt_rows, nxt, h_prev=h)
    return (_combine(pt_p, y, x, mod, layer, SEQ, 0, 0, n_prompt_rows, nxt),
            _combine(pt_s, y, x, mod, layer, DEC_SEQ, n_prompt_rows, slots_p, n_prompt_rows, nxt))


def kernel(x_prompt, x_sample, cache_k_0, cache_v_0, cache_k_1, cache_v_1, cache_k_3, cache_v_3, c, c_ctx, norm1_g, norm2_g, w_mod, b_mod, a_w_qkv, a_sink, a_w_o, b_w_qkv, b_rpb, b_w_o, c_w_in, c_v_g, c_w_s, c_b_s, c_w_out, w_router, w_gate, w_up, w_down, final_g):
    n_prompt, _, d = x_prompt.shape
    n_latent = x_sample.shape[0]
    depth = w_mod.shape[0]
    p_rows = n_prompt * SEQ
    s_rows = n_latent * DEC_SEQ
    caches = {0: (cache_k_0, cache_v_0), 1: (cache_k_1, cache_v_1), 3: (cache_k_3, cache_v_3)}

    cond = jnp.concatenate([c_ctx[None], c, jnp.zeros((COND_ROWS - 1 - n_latent, d), F32)], axis=0)
    mod = _adaln_all(cond, w_mod, b_mod).reshape(depth * COND_ROWS * 6, 1, d)
    g1 = norm1_g.reshape(depth, 1, d)
    g2 = norm2_g.reshape(depth, 1, d)
    x = (x_prompt.reshape(p_rows, d), x_sample.reshape(s_rows, d))
    h = _enter_stream(x[0], 0, p_rows + s_rows, g1, mod, p_rows)
    h = _enter_stream(x[1], p_rows, p_rows + s_rows, g1, mod, p_rows, prev=h)

    new_state = []
    for l in range(depth):
        kind, j = l % 3, l // 3
        if kind == 0:
            qkv = _matmul(h, a_w_qkv, j, F32)
            o, k_new, v_new = _attn_ctx(qkv, a_sink[j], p_rows + s_rows, n_prompt, N_KV_HEADS)
            o = _attn_win(qkv, *caches[l], a_sink[j], o, p_rows, n_latent)
            w_out = a_w_o[j]
            new_state += [k_new, v_new]
        elif kind == 1:
            qkv = _matmul(h, b_w_qkv, j, F32)
            o, k_new, v_new = _attn_ctx(qkv, None, p_rows + s_rows, n_prompt, N_HEADS)
            o = _attn_na(qkv, *caches[l], b_rpb[j], o, p_rows, n_latent)
            w_out = b_w_o[j]
            new_state += [k_new, v_new]
        else:
            z = _matmul(h, c_w_in, j, BF16, act="gelu")
            o = _sgu(z, c_v_g[j], c_w_s[j], c_b_s[j])
            w_out = c_w_out[j]
        x, h = _matmul_residual(o, w_out.astype(BF16), x, mod, g2, l, p_rows)
        if l + 1 < depth:
            x, h = _moe(h, x, mod, l, w_router, w_gate, w_up, w_down, n_prompt, n_latent, (g1, l + 1))
    y_prompt, y_sample = _moe(h, x, mod, depth - 1, w_router, w_gate, w_up, w_down, n_prompt, n_latent,
                              final_g.reshape(1, d))
    return (y_prompt.reshape(n_prompt, SEQ, d), y_sample.reshape(n_latent, DEC_SEQ, d), *new_state)
```
